```python
import math
import jax, jax.numpy as jnp
from jax import lax
import numpy as np

D_MODEL = 4096
BATCH = 1
SEQ = 8192
DEPTH = 2
DEC_BATCH = 8
DEC_SEQ = 2048
PAST_LEN = 128

N_EVEN = (DEPTH + 1) // 2
N_ODD = DEPTH // 2

POOL_WINDOWS = (2, 4, 8, 16)
N_POOL = len(POOL_WINDOWS)
POOL_DIM = (3 * D_MODEL) // 4
POOL_GROUP = POOL_DIM // N_POOL
FOURIER_DIM = D_MODEL - POOL_DIM
N_FOURIER_HEADS = 4
FOURIER_HEAD_DIM = FOURIER_DIM // N_FOURIER_HEADS

DIFF_HEAD_DIM = 128
N_DIFF_HEADS = D_MODEL // (2 * DIFF_HEAD_DIM)
ROT_DIM = DIFF_HEAD_DIM // 4
ROPE_THETA = 500000.0
Q_BLOCK = 128
SUBLN_EPS = 1e-5

D_FF = 11008
CONV_WIDTH = 3
NORM_EPS = 1e-6

kernel_name = 'hybrid_pool_fourier_diffattn_encoder'


def rms_norm(x, g, eps=NORM_EPS):
    xf = x.astype(jnp.float32)
    y = xf * lax.rsqrt(jnp.mean(xf * xf, axis=-1, keepdims=True) + eps)
    return (y * g.astype(jnp.float32)).astype(x.dtype)


def pool_mixer(u, pool_w, pool_scale):
    B, L, _ = u.shape
    ug = u.reshape(B, L, N_POOL, POOL_GROUP).astype(jnp.float32)
    cs = jnp.pad(jnp.cumsum(ug, axis=1), ((0, 0), (1, 0), (0, 0), (0, 0)))
    t = jnp.arange(L)
    outs = []
    for g, w in enumerate(POOL_WINDOWS):
        h = w // 2
        csg = jnp.pad(cs[:, :, g], ((0, 0), (h, h), (0, 0)), mode='edge')
        win = csg[:, 2 * h:2 * h + L] - csg[:, :L]
        cnt = (jnp.minimum(t + h, L) - jnp.maximum(t - h, 0)).astype(jnp.float32)
        outs.append(win / cnt[None, :, None] - ug[:, :, g])
    pooled = jnp.stack(outs, axis=2).astype(u.dtype)
    mixed = jnp.einsum('blgc,gcd->blgd', pooled, pool_w).reshape(B, L, POOL_DIM)
    return mixed * pool_scale


def fourier_mixer(v, fourier_w):
    B, L, _ = v.shape
    vh = v.reshape(B, L, N_FOURIER_HEADS, FOURIER_HEAD_DIM).astype(jnp.float32)
    f = jnp.fft.fft2(vh, axes=(1, 3), norm='ortho').real
    return f.reshape(B, L, FOURIER_DIM).astype(v.dtype) @ fourier_w


def even_mixer(x, norm_g, w_in, pool_w, pool_scale, fourier_w, w_out):
    h = rms_norm(x, norm_g)
    u = h @ w_in
    a = pool_mixer(u[..., :POOL_DIM], pool_w, pool_scale)
    b = fourier_mixer(u[..., POOL_DIM:], fourier_w)
    return jnp.concatenate([a, b], axis=-1) @ w_out


def partial_rotary(x, pos):
    inv = ROPE_THETA ** (-jnp.arange(0, ROT_DIM, 2, dtype=jnp.float32) / ROT_DIM)
    ang = pos[:, None] * inv[None, :]
    cos = jnp.concatenate([jnp.cos(ang), jnp.cos(ang)], axis=-1)
    sin = jnp.concatenate([jnp.sin(ang), jnp.sin(ang)], axis=-1)
    xr = x[..., :ROT_DIM].astype(jnp.float32)
    x1, x2 = xr[..., :ROT_DIM // 2], xr[..., ROT_DIM // 2:]
    rot = jnp.concatenate([-x2, x1], axis=-1)
    xr = xr * cos + rot * sin
    return jnp.concatenate([xr.astype(x.dtype), x[..., ROT_DIM:]], axis=-1)


def diff_attention(x, norm_g, w_qkv, lq1, lk1, lq2, lk2, subln_w, w_out, layer):
    B, L, _ = x.shape
    h = rms_norm(x, norm_g)
    qkv = h @ w_qkv
    q, k, v = jnp.split(qkv, 3, axis=-1)
    q = q.reshape(B, L, N_DIFF_HEADS, 2, DIFF_HEAD_DIM).transpose(0, 2, 3, 1, 4)
    k = k.reshape(B, L, N_DIFF_HEADS, 2, DIFF_HEAD_DIM).transpose(0, 2, 3, 1, 4)
    v = v.reshape(B, L, N_DIFF_HEADS, 2 * DIFF_HEAD_DIM).transpose(0, 2, 1, 3)
    pos = jnp.arange(L, dtype=jnp.float32)
    q = partial_rotary(q, pos)
    k = partial_rotary(k, pos)
    lambda_init = 0.8 - 0.6 * math.exp(-0.3 * layer)
    lam = (jnp.exp(jnp.sum(lq1.astype(jnp.float32) * lk1.astype(jnp.float32)))
           - jnp.exp(jnp.sum(lq2.astype(jnp.float32) * lk2.astype(jnp.float32)))
           + lambda_init)
    scale = DIFF_HEAD_DIM ** -0.5
    vf = v.astype(jnp.float32)
    nb = L // Q_BLOCK
    qb = jnp.moveaxis(q.reshape(B, N_DIFF_HEADS, 2, nb, Q_BLOCK, DIFF_HEAD_DIM), 3, 0)

    def block(qi):
        s = jnp.einsum('bhmqd,bhmkd->bhmqk', qi, k, preferred_element_type=jnp.float32) * scale
        p = jax.nn.softmax(s, axis=-1)
        a = p[:, :, 0] - lam * p[:, :, 1]
        return jnp.einsum('bhqk,bhkv->bhqv', a, vf)

    o = lax.map(block, qb)
    o = jnp.moveaxis(o, 0, 2).reshape(B, N_DIFF_HEADS, L, 2 * DIFF_HEAD_DIM)
    o = rms_norm(o, subln_w, eps=SUBLN_EPS) * (1.0 - lambda_init)
    o = o.transpose(0, 2, 1, 3).reshape(B, L, D_MODEL).astype(x.dtype)
    return o @ w_out


def conv_ffn(x, norm_g, w_up, conv_w, conv_b, w_down):
    h = rms_norm(x, norm_g)
    u = h @ w_up
    p = jnp.pad(u, ((0, 0), (1, 1), (0, 0)))
    c = p[:, :-2] * conv_w[0] + p[:, 1:-1] * conv_w[1] + p[:, 2:] * conv_w[2] + conv_b
    gate, up = jnp.split(c, 2, axis=-1)
    return (jax.nn.silu(gate) * up) @ w_down


def trunk(x, mix_norm_even, w_in_even, pool_w, pool_scale, fourier_w, w_out_even,
          mix_norm_odd, w_qkv, lambda_q1, lambda_k1, lambda_q2, lambda_k2, subln_w, w_out_odd,
          ffn_norm, w_up, conv_w, conv_b, w_down, final_norm):
    for layer in range(DEPTH):
        i = layer // 2
        if layer % 2 == 0:
            x = x + even_mixer(x, mix_norm_even[i], w_in_even[i], pool_w[i], pool_scale[i],
                               fourier_w[i], w_out_even[i])
        else:
            x = x + diff_attention(x, mix_norm_odd[i], w_qkv[i], lambda_q1[i], lambda_k1[i],
                                   lambda_q2[i], lambda_k2[i], subln_w[i], w_out_odd[i], layer)
        x = x + conv_ffn(x, ffn_norm[layer], w_up[layer], conv_w[layer], conv_b[layer], w_down[layer])
    return rms_norm(x, final_norm)


def setup_inputs(seed: int = 0) -> dict:
    key = jax.random.key(seed)
    ks = jax.random.split(key, 24)
    f32 = jnp.float32
    D = D_MODEL

    def nrm(k, shape, s):
        return jax.random.normal(k, shape, f32) * s

    return {
        'x_prompt': nrm(ks[0], (BATCH, SEQ, D), 1.0),
        'x_sample': nrm(ks[1], (DEC_BATCH, DEC_SEQ, D), 1.0),
        'mix_norm_even': 1.0 + nrm(ks[2], (N_EVEN, D), 0.02),
        'w_in_even': nrm(ks[3], (N_EVEN, D, D), D ** -0.5),
        'pool_w': nrm(ks[4], (N_EVEN, N_POOL, POOL_GROUP, POOL_GROUP), POOL_GROUP ** -0.5),
        'pool_scale': 1.0 + nrm(ks[5], (N_EVEN, POOL_DIM), 0.1),
        'fourier_w': nrm(ks[6], (N_EVEN, FOURIER_DIM, FOURIER_DIM), FOURIER_DIM ** -0.5),
        'w_out_even': nrm(ks[7], (N_EVEN, D, D), D ** -0.5),
        'mix_norm_odd': 1.0 + nrm(ks[8], (N_ODD, D), 0.02),
        'w_qkv': nrm(ks[9], (N_ODD, D, 3 * D), D ** -0.5),
        'lambda_q1': nrm(ks[10], (N_ODD, DIFF_HEAD_DIM), 0.1),
        'lambda_k1': nrm(ks[11], (N_ODD, DIFF_HEAD_DIM), 0.1),
        'lambda_q2': nrm(ks[12], (N_ODD, DIFF_HEAD_DIM), 0.1),
        'lambda_k2': nrm(ks[13], (N_ODD, DIFF_HEAD_DIM), 0.1),
        'subln_w': 1.0 + nrm(ks[14], (N_ODD, 2 * DIFF_HEAD_DIM), 0.02),
        'w_out_odd': nrm(ks[15], (N_ODD, D, D), D ** -0.5),
        'ffn_norm': 1.0 + nrm(ks[16], (DEPTH, D), 0.02),
        'w_up': nrm(ks[17], (DEPTH, D, 2 * D_FF), D ** -0.5),
        'conv_w': nrm(ks[18], (DEPTH, CONV_WIDTH, 2 * D_FF), CONV_WIDTH ** -0.5),
        'conv_b': nrm(ks[19], (DEPTH, 2 * D_FF), 0.01),
        'w_down': nrm(ks[20], (DEPTH, D_FF, D), D_FF ** -0.5),
        'final_norm': 1.0 + nrm(ks[21], (D,), 0.02),
    }


def reference(x_prompt, x_sample, mix_norm_even, w_in_even, pool_w, pool_scale, fourier_w,
              w_out_even, mix_norm_odd, w_qkv, lambda_q1, lambda_k1, lambda_q2, lambda_k2,
              subln_w, w_out_odd, ffn_norm, w_up, conv_w, conv_b, w_down, final_norm):
    y_prompt = trunk(x_prompt, mix_norm_even, w_in_even, pool_w, pool_scale, fourier_w, w_out_even,
                     mix_norm_odd, w_qkv, lambda_q1, lambda_k1, lambda_q2, lambda_k2, subln_w,
                     w_out_odd, ffn_norm, w_up, conv_w, conv_b, w_down, final_norm)
    y_sample = trunk(x_sample, mix_norm_even, w_in_even, pool_w, pool_scale, fourier_w, w_out_even,
                     mix_norm_odd, w_qkv, lambda_q1, lambda_k1, lambda_q2, lambda_k2, subln_w,
                     w_out_odd, ffn_norm, w_up, conv_w, conv_b, w_down, final_norm)
    return (y_prompt, y_sample)
```

```python
import functools
import math

import numpy as np
import jax
import jax.numpy as jnp
from jax import lax
from jax.experimental import pallas as pl
from jax.experimental.pallas import tpu as pltpu

F32 = jnp.float32
BF16 = jnp.bfloat16

POOL_WINDOWS = (2, 4, 8, 16)
FOURIER_HEAD_DIM = 256
ROPE_THETA = 500000.0
NORM_EPS = 1e-6
SUBLN_EPS = 1e-5
LAMBDA_INIT_DECAY = 0.3

LANES = 128
SUBLANES_F32 = 8
SUBLANES_BF16 = 16
VMEM_LIMIT_BYTES = 56 * 1024 * 1024

MAX_ROW_TILE = 1024
FFN_COL_TILE = 512
DFT_ROW_SPLIT = 64


def _params(*sem):
    return pltpu.CompilerParams(dimension_semantics=sem, vmem_limit_bytes=VMEM_LIMIT_BYTES)


class _Layout:
    def __init__(self, n_prompt, len_prompt, n_sample, len_sample):
        self.lp, self.ls = len_prompt, len_sample
        self.rows_p, self.rows_s = n_prompt * len_prompt, n_sample * len_sample
        self.rows = self.rows_p + self.rows_s
        self.tm = min(MAX_ROW_TILE, len_prompt, len_sample)
        assert len_prompt % self.tm == 0 and len_sample % self.tm == 0
        self.tiles_p = self.rows_p // self.tm
        self.tiles = self.rows // self.tm

    def seq_pos(self, i):
        in_p = i < self.tiles_p
        length = jnp.where(in_p, self.lp, self.ls)
        tile_in_seq = jnp.where(in_p, i % (self.lp // self.tm), (i - self.tiles_p) % (self.ls // self.tm))
        return length, tile_in_seq * self.tm


def _rmsnorm_kernel(x_ref, g_ref, o_ref, *, eps):
    x = x_ref[...]
    y = x * lax.rsqrt(jnp.mean(x * x, axis=-1, keepdims=True) + eps)
    o_ref[...] = (y * g_ref[...]).astype(o_ref.dtype)


def _rmsnorm(x, g, *, out_dtype, row_block0=0, rows=None, tm=256):
    d = x.shape[1]
    rows = x.shape[0] if rows is None else rows
    return pl.pallas_call(
        functools.partial(_rmsnorm_kernel, eps=NORM_EPS),
        grid=(rows // tm,),
        in_specs=[pl.BlockSpec((tm, d), lambda i: (i + row_block0, 0)),
                  pl.BlockSpec((1, d), lambda i: (0, 0))],
        out_specs=pl.BlockSpec((tm, d), lambda i: (i, 0)),
        out_shape=jax.ShapeDtypeStruct((rows, d), out_dtype),
        compiler_params=_params("parallel"),
        name="rmsnorm",
    )(x, g.reshape(1, d).astype(F32))


def _mm_kernel(a_ref, b_ref, *rest, has_resid):
    o_ref = rest[-1]
    acc = jnp.dot(a_ref[...], b_ref[...], preferred_element_type=F32)
    if has_resid:
        acc = acc + rest[0][...]
    o_ref[...] = acc.astype(o_ref.dtype)


def _matmul(a, b, *, out_dtype, tm, tn, resid=None, name="matmul"):
    m, k = a.shape
    n = b.shape[1]
    in_specs = [pl.BlockSpec((tm, k), lambda i, j: (i, 0)),
                pl.BlockSpec((k, tn), lambda i, j: (0, j))]
    args = [a, b]
    if resid is not None:
        in_specs.append(pl.BlockSpec((tm, tn), lambda i, j: (i, j)))
        args.append(resid)
    return pl.pallas_call(
        functools.partial(_mm_kernel, has_resid=resid is not None),
        grid=(m // tm, n // tn),
        in_specs=in_specs,
        out_specs=pl.BlockSpec((tm, tn), lambda i, j: (i, j)),
        out_shape=jax.ShapeDtypeStruct((m, n), out_dtype),
        compiler_params=_params("parallel", "arbitrary"),
        name=name,
    )(*args)


def _pool_kernel(up_ref, u_ref, un_ref, w_ref, s_ref, o_ref, *, lay, halo):
    g = pl.program_id(0)
    i = pl.program_id(1)
    tm = lay.tm
    length, pos0 = lay.seq_pos(i)
    x = u_ref[...]
    prev = jnp.where(pos0 == 0, 0.0, up_ref[...])
    nxt = jnp.where(pos0 + tm == length, 0.0, un_ref[...])
    e = jnp.concatenate([prev, x, nxt], axis=0)
    n = tm + 2 * halo
    s1 = e + pltpu.roll(e, 1, 0)
    t = pos0 + lax.broadcasted_iota(jnp.int32, (tm, 1), 0)

    for gi, w in enumerate(POOL_WINDOWS):
        h = w // 2

        @pl.when(g == gi)
        def _(h=h):
            s, k = s1, 1
            while k < h:
                s = pltpu.roll(s, k, 0) + pltpu.roll(s, n - k, 0)
                k *= 2
            win = s[halo:halo + tm]
            cnt = (jnp.minimum(t + h, length) - jnp.maximum(t - h, 0)).astype(F32)
            pooled = win / cnt - x
            y = jnp.dot(pooled.astype(BF16), w_ref[...], preferred_element_type=F32)
            o_ref[...] = (y * s_ref[...]).astype(o_ref.dtype)


def _pool_mixer(u, pool_w, pool_scale, lay):
    n_groups, c, _ = pool_w.shape
    tm, halo = lay.tm, SUBLANES_F32
    assert max(POOL_WINDOWS) // 2 <= halo and len(POOL_WINDOWS) == n_groups
    hb = tm // halo
    last_hb = lay.rows // halo - 1
    return pl.pallas_call(
        functools.partial(_pool_kernel, lay=lay, halo=halo),
        grid=(n_groups, lay.tiles),
        in_specs=[
            pl.BlockSpec((halo, c), lambda g, i: (jnp.maximum(i * hb - 1, 0), g)),
            pl.BlockSpec((tm, c), lambda g, i: (i, g)),
            pl.BlockSpec((halo, c), lambda g, i: (jnp.minimum((i + 1) * hb, last_hb), g)),
            pl.BlockSpec((None, c, c), lambda g, i: (g, 0, 0)),
            pl.BlockSpec((1, c), lambda g, i: (0, g)),
        ],
        out_specs=pl.BlockSpec((tm, c), lambda g, i: (i, g)),
        out_shape=jax.ShapeDtypeStruct((lay.rows, n_groups * c), BF16),
        compiler_params=_params("parallel", "parallel"),
        name="pool_mixer",
    )(u, u, u, pool_w, pool_scale.reshape(1, n_groups * c).astype(F32))


def _chan_dft_kernel(u_ref, w_ref, a_ref, b_ref, *, n):
    ab = jnp.dot(u_ref[...].astype(BF16), w_ref[...], preferred_element_type=F32)
    a_ref[...] = ab[:, :n].astype(a_ref.dtype)
    b_ref[...] = ab[:, n:].astype(b_ref.dtype)


def _chan_dft(u, col0, heads, lay):
    n = FOURIER_HEAD_DIM
    jk = np.outer(np.arange(n), np.arange(n)) % n
    ang = 2.0 * np.pi * jk / n
    w = jnp.asarray(np.concatenate([np.cos(ang), np.sin(ang)], axis=1) / math.sqrt(n), dtype=BF16)
    cb0 = col0 // n
    out = jax.ShapeDtypeStruct((lay.rows, heads * n), BF16)
    return pl.pallas_call(
        functools.partial(_chan_dft_kernel, n=n),
        grid=(heads, lay.tiles),
        in_specs=[pl.BlockSpec((lay.tm, n), lambda h, i: (i, cb0 + h)),
                  pl.BlockSpec((n, 2 * n), lambda h, i: (0, 0))],
        out_specs=[pl.BlockSpec((lay.tm, n), lambda h, i: (i, h))] * 2,
        out_shape=[out, out],
        compiler_params=_params("parallel", "parallel"),
        name="chan_dft",
    )(u, w)


def _trig_rows_kernel(c_ref, s_ref, *, length, mult, tr, tc):
    r = (lax.broadcasted_iota(jnp.int32, (tr, tc), 0) + pl.program_id(0) * tr) * mult
    c = lax.broadcasted_iota(jnp.int32, (tr, tc), 1) + pl.program_id(1) * tc
    ang = ((r * c) & (length - 1)).astype(F32) * (2.0 * math.pi / length)
    c_ref[...] = jnp.cos(ang)
    s_ref[...] = jnp.sin(ang)


def _trig_rows(length, n_rows, mult):
    tr, tc = SUBLANES_F32, min(length, 2048)
    out = jax.ShapeDtypeStruct((n_rows, length), F32)
    return pl.pallas_call(
        functools.partial(_trig_rows_kernel, length=length, mult=mult, tr=tr, tc=tc),
        grid=(n_rows // tr, length // tc),
        out_specs=[pl.BlockSpec((tr, tc), lambda i, j: (i, j))] * 2,
        out_shape=[out, out],
        compiler_params=_params("parallel", "parallel"),
        name="trig_rows",
    )()


def _dft_mat_kernel(ca_ref, sa_ref, cb_ref, sb_ref, c_ref, s_ref, *, split, groups):
    cb, sb = cb_ref[...], sb_ref[...]
    for q in range(groups):
        ca, sa = ca_ref[q:q + 1, :], sa_ref[q:q + 1, :]
        rows = slice(q * split, (q + 1) * split)
        c_ref[rows, :] = (ca * cb - sa * sb).astype(c_ref.dtype)
        s_ref[rows, :] = (-(sa * cb + ca * sb)).astype(s_ref.dtype)


def _dft_matrices(length):
    assert length & (length - 1) == 0
    split = min(DFT_ROW_SPLIT, length // SUBLANES_F32)
    groups = SUBLANES_F32
    ca, sa = _trig_rows(length, length // split, split)
    cb, sb = _trig_rows(length, split, 1)
    tc = min(length, 2048)
    out = jax.ShapeDtypeStruct((length, length), BF16)
    return pl.pallas_call(
        functools.partial(_dft_mat_kernel, split=split, groups=groups),
        grid=(length // (split * groups), length // tc),
        in_specs=[pl.BlockSpec((groups, tc), lambda i, j: (i, j))] * 2
        + [pl.BlockSpec((split, tc), lambda i, j: (0, j))] * 2,
        out_specs=[pl.BlockSpec((split * groups, tc), lambda i, j: (i, j))] * 2,
        out_shape=[out, out],
        compiler_params=_params("parallel", "parallel"),
        name="dft_matrices",
    )(ca, sa, cb, sb)


def _seq_dft_kernel(c_ref, s_ref, a_ref, b_ref, o_ref, acc_ref, *, scale):
    k = pl.program_id(2)

    @pl.when(k == 0)
    def _():
        acc_ref[...] = jnp.zeros_like(acc_ref)

    acc_ref[...] += (jnp.dot(c_ref[...], a_ref[...], preferred_element_type=F32)
                     + jnp.dot(s_ref[...], b_ref[...], preferred_element_type=F32))

    @pl.when(k == pl.num_programs(2) - 1)
    def _():
        o_ref[...] = (acc_ref[...] * scale).astype(o_ref.dtype)


def _seq_dft(va, vb, row0, n_seq, length, tile):
    cmat, nsmat = _dft_matrices(length)
    n = va.shape[1]
    tm = tk = min(tile, length)
    nt = length // tm
    kb0 = row0 // tk
    return pl.pallas_call(
        functools.partial(_seq_dft_kernel, scale=1.0 / math.sqrt(length)),
        grid=(n_seq, nt, nt),
        in_specs=[pl.BlockSpec((tm, tk), lambda s, i, k: (i, k)),
                  pl.BlockSpec((tm, tk), lambda s, i, k: (i, k)),
                  pl.BlockSpec((tk, n), lambda s, i, k: (kb0 + s * nt + k, 0)),
                  pl.BlockSpec((tk, n), lambda s, i, k: (kb0 + s * nt + k, 0))],
        out_specs=pl.BlockSpec((tm, n), lambda s, i, k: (s * nt + i, 0)),
        out_shape=jax.ShapeDtypeStruct((n_seq * length, n), BF16),
        scratch_shapes=[pltpu.VMEM((tm, n), F32)],
        compiler_params=_params("parallel", "parallel", "arbitrary"),
        name="seq_dft",
    )(cmat, nsmat, va, vb)


def _out_proj2_kernel(a_ref, b_ref, w1_ref, w2_ref, r_ref, o_ref):
    acc = jnp.dot(a_ref[...], w1_ref[...], preferred_element_type=F32)
    acc = acc + jnp.dot(b_ref[...], w2_ref[...], preferred_element_type=F32)
    o_ref[...] = acc + r_ref[...]


def _out_proj2(a, b, w, resid, *, tm, tn):
    m, ka = a.shape
    kb = b.shape[1]
    n = w.shape[1]
    assert ka % kb == 0
    return pl.pallas_call(
        _out_proj2_kernel,
        grid=(m // tm, n // tn),
        in_specs=[pl.BlockSpec((tm, ka), lambda i, j: (i, 0)),
                  pl.BlockSpec((tm, kb), lambda i, j: (i, 0)),
                  pl.BlockSpec((ka, tn), lambda i, j: (0, j)),
                  pl.BlockSpec((kb, tn), lambda i, j: (ka // kb, j)),
                  pl.BlockSpec((tm, tn), lambda i, j: (i, j))],
        out_specs=pl.BlockSpec((tm, tn), lambda i, j: (i, j)),
        out_shape=jax.ShapeDtypeStruct((m, n), F32),
        compiler_params=_params("parallel", "arbitrary"),
        name="even_out_proj",
    )(a, b, w, w, resid)


def _rope_tables_kernel(inv_ref, cos_ref, sa_ref, sb_ref, *, tr, half):
    pos = (lax.broadcasted_iota(jnp.int32, (tr, LANES), 0) + pl.program_id(0) * tr).astype(F32)
    lane = lax.broadcasted_iota(jnp.int32, (tr, LANES), 1)
    ang = pos * inv_ref[...]
    c, s = jnp.cos(ang), jnp.sin(ang)
    cos_ref[...] = jnp.where(lane < 2 * half, c, 1.0)
    sa_ref[...] = jnp.where((lane >= half) & (lane < 2 * half), s, 0.0)
    sb_ref[...] = jnp.where(lane < half, -s, 0.0)


def _rope_tables(length, head_dim):
    assert head_dim == LANES
    rot = head_dim // 4
    half = rot // 2
    inv = np.ones((1, LANES), np.float64)
    freqs = ROPE_THETA ** (-np.arange(0, rot, 2, dtype=np.float64) / rot)
    inv[0, :half] = freqs
    inv[0, half:rot] = freqs
    tr = min(length, 512)
    out = jax.ShapeDtypeStruct((length, LANES), F32)
    return pl.pallas_call(
        functools.partial(_rope_tables_kernel, tr=tr, half=half),
        grid=(length // tr,),
        in_specs=[pl.BlockSpec((1, LANES), lambda i: (0, 0))],
        out_specs=[pl.BlockSpec((tr, LANES), lambda i: (i, 0))] * 3,
        out_shape=[out, out, out],
        compiler_params=_params("parallel"),
        name="rope_tables",
    )(jnp.asarray(inv, dtype=F32))


def _qkv_kernel(h_ref, w_ref, cos_ref, sa_ref, sb_ref, o_ref, *, rope_tiles, half):
    y = jnp.dot(h_ref[...], w_ref[...], preferred_element_type=F32)
    j = pl.program_id(1)

    @pl.when(j < rope_tiles)
    def _():
        cos, sa, sb = cos_ref[...], sa_ref[...], sb_ref[...]
        for c in range(y.shape[1] // LANES):
            yc = y[:, c * LANES:(c + 1) * LANES]
            out = yc * cos + pltpu.roll(yc, half, 1) * sa + pltpu.roll(yc, LANES - half, 1) * sb
            o_ref[:, c * LANES:(c + 1) * LANES] = out.astype(o_ref.dtype)

    @pl.when(j >= rope_tiles)
    def _():
        o_ref[...] = y.astype(o_ref.dtype)


def _qkv_proj(h, w, tables, lay, head_dim, *, tn):
    m, k = h.shape
    n = w.shape[1]
    tm = lay.tm

    def pos_block(i, j):
        in_p = i < lay.tiles_p
        return jnp.where(in_p, i % (lay.lp // tm), (i - lay.tiles_p) % (lay.ls // tm)), 0

    tab = pl.BlockSpec((tm, LANES), pos_block)
    return pl.pallas_call(
        functools.partial(_qkv_kernel, rope_tiles=(2 * n // 3) // tn, half=head_dim // 8),
        grid=(m // tm, n // tn),
        in_specs=[pl.BlockSpec((tm, k), lambda i, j: (i, 0)),
                  pl.BlockSpec((k, tn), lambda i, j: (0, j)),
                  tab, tab, tab],
        out_specs=pl.BlockSpec((tm, tn), lambda i, j: (i, j)),
        out_shape=jax.ShapeDtypeStruct((m, n), BF16),
        compiler_params=_params("parallel", "arbitrary"),
        name="qkv_proj",
    )(h, w, *tables)


def _attn_kernel(lq1_ref, lk1_ref, lq2_ref, lk2_ref, q_ref, k_ref, v_ref, w_ref, o_ref, *, hd, scale, lam_init):
    lam = (jnp.exp(jnp.sum(lq1_ref[...] * lk1_ref[...], keepdims=True))
           - jnp.exp(jnp.sum(lq2_ref[...] * lk2_ref[...], keepdims=True)) + lam_init)
    q, k = q_ref[...], k_ref[...]
    dn = (((1,), (1,)), ((), ()))

    def probs(m):
        s = lax.dot_general(q[:, m * hd:(m + 1) * hd], k[:, m * hd:(m + 1) * hd], dn,
                            preferred_element_type=F32) * scale
        p = jnp.exp(s - jnp.max(s, axis=-1, keepdims=True))
        return p, jnp.sum(p, axis=-1, keepdims=True)

    p1, l1 = probs(0)
    p2, l2 = probs(1)
    a = p1 * (1.0 / l1) - p2 * (lam / l2)
    o = jnp.dot(a.astype(BF16), v_ref[...], preferred_element_type=F32)
    o = o * lax.rsqrt(jnp.mean(o * o, axis=-1, keepdims=True) + SUBLN_EPS)
    o_ref[...] = (o * w_ref[...] * (1.0 - lam_init)).astype(o_ref.dtype)


def _diff_attention(qkv, lams, subln_w, row0, n_seq, length, d_model, hd, layer, *, tq):
    heads = d_model // (2 * hd)
    lam_init = 0.8 - 0.6 * math.exp(-LAMBDA_INIT_DECAY * layer)
    qt = length // tq
    qb0, sb0 = row0 // tq, row0 // length
    vec = pl.BlockSpec((1, hd), lambda b, h, i: (0, 0))
    kv_mode = pl.Buffered(1)
    return pl.pallas_call(
        functools.partial(_attn_kernel, hd=hd, scale=hd ** -0.5, lam_init=lam_init),
        grid=(n_seq, heads, qt),
        in_specs=[vec, vec, vec, vec,
                  pl.BlockSpec((tq, 2 * hd), lambda b, h, i: (qb0 + b * qt + i, h)),
                  pl.BlockSpec((length, 2 * hd), lambda b, h, i: (sb0 + b, heads + h), pipeline_mode=kv_mode),
                  pl.BlockSpec((length, 2 * hd), lambda b, h, i: (sb0 + b, 2 * heads + h), pipeline_mode=kv_mode),
                  pl.BlockSpec((1, 2 * hd), lambda b, h, i: (0, 0))],
        out_specs=pl.BlockSpec((tq, 2 * hd), lambda b, h, i: (b * qt + i, h)),
        out_shape=jax.ShapeDtypeStruct((n_seq * length, d_model), BF16),
        compiler_params=_params("parallel", "parallel", "arbitrary"),
        name="diff_attention",
    )(*[v.reshape(1, hd).astype(F32) for v in lams], qkv, qkv, qkv, subln_w.reshape(1, 2 * hd).astype(F32))


def _ffn_up_kernel(hp_ref, h_ref, hn_ref, w_ref, cw_ref, cb_ref, g_ref, lhs_ref, *, lay, halo, tn):
    i = pl.program_id(0)
    tm = lay.tm

    @pl.when(pl.program_id(1) == 0)
    def _():
        length, pos0 = lay.seq_pos(i)
        zero = jnp.zeros(hp_ref.shape, hp_ref.dtype)
        lhs_ref[0:halo, :] = jnp.where(pos0 == 0, zero, hp_ref[...])
        lhs_ref[halo:halo + tm, :] = h_ref[...]
        lhs_ref[halo + tm:, :] = jnp.where(pos0 + tm == length, zero, hn_ref[...])

    n = tm + 2 * halo
    u = jnp.dot(lhs_ref[...], w_ref[...], preferred_element_type=F32)
    cw = cw_ref[...]
    c = (pltpu.roll(u, 1, 0)[halo:halo + tm] * cw[0:1] + u[halo:halo + tm] * cw[1:2]
         + pltpu.roll(u, n - 1, 0)[halo:halo + tm] * cw[2:3] + cb_ref[...])
    gate, up = c[:, :tn], c[:, tn:]
    g_ref[...] = (gate / (1.0 + jnp.exp(-gate)) * up).astype(g_ref.dtype)


def _interleave_gate_up(x, f, f_pad, tn):
    lead = x.shape[:-1]
    x = x.reshape(lead + (2, f))
    x = jnp.pad(x, [(0, 0)] * len(lead) + [(0, 0), (0, f_pad - f)])
    x = x.reshape(lead + (2, f_pad // tn, tn))
    x = jnp.swapaxes(x, -3, -2)
    return x.reshape(lead + (2 * f_pad,))


def _ffn_up(h, w_gu, conv_w, conv_b, lay, *, tn):
    m, k = h.shape
    f_pad = w_gu.shape[1] // 2
    tm, halo = lay.tm, SUBLANES_BF16
    hb = tm // halo
    last_hb = m // halo - 1
    return pl.pallas_call(
        functools.partial(_ffn_up_kernel, lay=lay, halo=halo, tn=tn),
        grid=(m // tm, f_pad // tn),
        in_specs=[pl.BlockSpec((halo, k), lambda i, j: (jnp.maximum(i * hb - 1, 0), 0)),
                  pl.BlockSpec((tm, k), lambda i, j: (i, 0)),
                  pl.BlockSpec((halo, k), lambda i, j: (jnp.minimum((i + 1) * hb, last_hb), 0)),
                  pl.BlockSpec((k, 2 * tn), lambda i, j: (0, j)),
                  pl.BlockSpec((3, 2 * tn), lambda i, j: (0, j)),
                  pl.BlockSpec((1, 2 * tn), lambda i, j: (0, j))],
        out_specs=pl.BlockSpec((tm, tn), lambda i, j: (i, j)),
        out_shape=jax.ShapeDtypeStruct((m, f_pad), BF16),
        scratch_shapes=[pltpu.VMEM((tm + 2 * halo, k), BF16)],
        compiler_params=_params("parallel", "arbitrary"),
        name="ffn_up",
    )(h, h, h, w_gu, conv_w, conv_b)


def _ffn_down_kernel(g_ref, w_ref, r_ref, o_ref):
    @pl.when(pl.program_id(2) == 0)
    def _():
        o_ref[...] = r_ref[...]

    o_ref[...] += jnp.dot(g_ref[...], w_ref[...], preferred_element_type=F32)


def _ffn_down(g, w, resid, *, tm, tn, tk):
    m, k = g.shape
    n = w.shape[1]
    return pl.pallas_call(
        _ffn_down_kernel,
        grid=(m // tm, n // tn, k // tk),
        in_specs=[pl.BlockSpec((tm, tk), lambda i, j, kk: (i, kk)),
                  pl.BlockSpec((tk, tn), lambda i, j, kk: (kk, j)),
                  pl.BlockSpec((tm, tn), lambda i, j, kk: (i, j))],
        out_specs=pl.BlockSpec((tm, tn), lambda i, j, kk: (i, j)),
        out_shape=jax.ShapeDtypeStruct((m, n), F32),
        compiler_params=_params("parallel", "parallel", "arbitrary"),
        name="ffn_down",
    )(g, w, resid)


def _down_k_tile(f_pad):
    units = f_pad // 256
    best = 1
    for d in range(1, units + 1):
        if units % d == 0 and d * 256 * 4 <= max(f_pad, 1024):
            best = d
    return best * 256


def kernel(x_prompt, x_sample, mix_norm_even, w_in_even, pool_w, pool_scale, fourier_w, w_out_even, mix_norm_odd,
           w_qkv, lambda_q1, lambda_k1, lambda_q2, lambda_k2, subln_w, w_out_odd, ffn_norm, w_up, conv_w, conv_b,
           w_down, final_norm):
    bp, lp, d = x_prompt.shape
    bs, ls, _ = x_sample.shape
    assert bp == 1, "the prompt group is handled as one sequence"
    lay = _Layout(bp, lp, bs, ls)
    tm = lay.tm
    depth = ffn_norm.shape[0]
    f = w_down.shape[1]
    tn_ffn = FFN_COL_TILE
    f_pad = -(-f // tn_ffn) * tn_ffn
    hd = lambda_q1.shape[1]
    pool_dim = pool_w.shape[1] * pool_w.shape[2]
    fourier_dim = fourier_w.shape[1]
    col_tile = min(512, d)

    x = jnp.concatenate([x_prompt.reshape(lp, d), x_sample.reshape(bs * ls, d)], axis=0)
    rope = None

    for layer in range(depth):
        i = layer // 2
        if layer % 2 == 0:
            h = _rmsnorm(x, mix_norm_even[i], out_dtype=BF16)
            u = _matmul(h, w_in_even[i].astype(BF16), out_dtype=F32, tm=tm, tn=col_tile, name="even_in_proj")
            a = _pool_mixer(u, pool_w[i].astype(BF16), pool_scale[i], lay)
            va, vb = _chan_dft(u, pool_dim, fourier_dim // FOURIER_HEAD_DIM, lay)
            fr = jnp.concatenate([_seq_dft(va, vb, 0, bp, lp, MAX_ROW_TILE),
                                  _seq_dft(va, vb, lay.rows_p, bs, ls, MAX_ROW_TILE)], axis=0)
            b = _matmul(fr, fourier_w[i].astype(BF16), out_dtype=BF16, tm=tm, tn=col_tile, name="fourier_proj")
            x = _out_proj2(a, b, w_out_even[i].astype(BF16), x, tm=tm, tn=col_tile)
        else:
            if rope is None:
                rope = _rope_tables(max(lp, ls), hd)
            h = _rmsnorm(x, mix_norm_odd[i], out_dtype=BF16)
            qkv = _qkv_proj(h, w_qkv[i].astype(BF16), rope, lay, hd, tn=col_tile)
            lams = (lambda_q1[i], lambda_k1[i], lambda_q2[i], lambda_k2[i])
            o = jnp.concatenate(
                [_diff_attention(qkv, lams, subln_w[i], 0, bp, lp, d, hd, layer, tq=min(128, lp)),
                 _diff_attention(qkv, lams, subln_w[i], lay.rows_p, bs, ls, d, hd, layer, tq=min(512, ls))], axis=0)
            x = _matmul(o, w_out_odd[i].astype(BF16), out_dtype=F32, tm=tm, tn=col_tile, resid=x, name="odd_out_proj")

        h = _rmsnorm(x, ffn_norm[layer], out_dtype=BF16)
        w_gu = _interleave_gate_up(w_up[layer].astype(BF16), f, f_pad, tn_ffn)
        cw = _interleave_gate_up(conv_w[layer].astype(F32), f, f_pad, tn_ffn)
        cb = _interleave_gate_up(conv_b[layer].astype(F32).reshape(1, 2 * f), f, f_pad, tn_ffn)
        g = _ffn_up(h, w_gu, cw, cb, lay, tn=tn_ffn)
        wd = jnp.pad(w_down[layer].astype(BF16), ((0, f_pad - f), (0, 0)))
        x = _ffn_down(g, wd, x, tm=tm, tn=min(1024, d), tk=_down_k_tile(f_pad))

    y_prompt = _rmsnorm(x, final_norm, out_dtype=F32, row_block0=0, rows=lay.rows_p)
    y_sample = _rmsnorm(x, final_norm, out_dtype=F32, row_block0=lay.rows_p // 256, rows=lay.rows_s)
    return y_prompt.reshape(bp, lp, d), y_sample.reshape(bs, ls, d)
```

```python
import functools
import math

import numpy as np
import jax
import jax.numpy as jnp
from jax import lax
from jax.experimental import pallas as pl
from jax.experimental.pallas import tpu as pltpu

F32 = jnp.float32
BF16 = jnp.bfloat16

POOL_WINDOWS = (2, 4, 8, 16)
FOURIER_HEAD_DIM = 256
ROPE_THETA = 500000.0
NORM_EPS = 1e-6
SUBLN_EPS = 1e-5
LAMBDA_INIT_DECAY = 0.3

LANES = 128
SUBLANES_F32 = 8
SUBLANES_BF16 = 16
MXU_DIM = 256
VMEM_LIMIT_BYTES = 56 * 1024 * 1024

MAX_ROW_TILE = 1024
NORM_ROW_TILE = 256
FFN_COL_TILE = 512
QKV_COL_TILE = 1024
ATTN_KV_CHUNK = 512
DFT_ROW_SPLIT = 64
LOG2_E = math.log2(math.e)


def _params(*sem):
    return pltpu.CompilerParams(dimension_semantics=sem, vmem_limit_bytes=VMEM_LIMIT_BYTES)


class _Layout:
    def __init__(self, n_prompt, len_prompt, n_sample, len_sample):
        self.lp, self.ls = len_prompt, len_sample
        self.rows_p, self.rows_s = n_prompt * len_prompt, n_sample * len_sample
        self.rows = self.rows_p + self.rows_s
        self.tm = min(MAX_ROW_TILE, len_prompt, len_sample)
        assert len_prompt % self.tm == 0 and len_sample % self.tm == 0
        self.tiles_p = self.rows_p // self.tm
        self.tiles = self.rows // self.tm

    def seq_pos(self, i):
        in_p = i < self.tiles_p
        length = jnp.where(in_p, self.lp, self.ls)
        tile_in_seq = jnp.where(in_p, i % (self.lp // self.tm), (i - self.tiles_p) % (self.ls // self.tm))
        return length, tile_in_seq * self.tm


def _into(dst):
    if dst is None:
        return [], [], None
    return [pl.BlockSpec(memory_space=pl.ANY)], [dst], dst


def _rmsnorm_kernel(x_ref, g_ref, *rest, eps):
    o_ref = rest[-1]
    x = x_ref[...]
    y = x * lax.rsqrt(jnp.mean(x * x, axis=-1, keepdims=True) + eps)
    o_ref[...] = (y * g_ref[...]).astype(o_ref.dtype)


def _rmsnorm(x, g, *, out_dtype, in_row0=0, rows=None, out_rows=None, out_row0=0, dst=None):
    d = x.shape[1]
    tm = NORM_ROW_TILE
    rows = x.shape[0] if rows is None else rows
    out_rows = rows if out_rows is None else out_rows
    ib0, ob0 = in_row0 // tm, out_row0 // tm
    extra_specs, extra_args, _ = _into(dst)
    return pl.pallas_call(
        functools.partial(_rmsnorm_kernel, eps=NORM_EPS),
        grid=(rows // tm,),
        in_specs=[pl.BlockSpec((tm, d), lambda i: (i + ib0, 0)),
                  pl.BlockSpec((1, d), lambda i: (0, 0))] + extra_specs,
        out_specs=pl.BlockSpec((tm, d), lambda i: (i + ob0, 0)),
        out_shape=jax.ShapeDtypeStruct((out_rows, d), out_dtype),
        input_output_aliases={2: 0} if dst is not None else {},
        compiler_params=_params("parallel"),
        name="rmsnorm",
    )(x, g.reshape(1, d).astype(F32), *extra_args)


def _rmsnorm_parts(parts, g, total_rows):
    h = None
    for arr, row0 in parts:
        h = _rmsnorm(arr, g, out_dtype=BF16, out_rows=total_rows, out_row0=row0, dst=h)
    return h


def _mm_kernel(a_ref, b_ref, *rest, has_resid):
    o_ref = rest[-1]
    acc = jnp.dot(a_ref[...], b_ref[...], preferred_element_type=F32)
    if has_resid:
        acc = acc + rest[0][...]
    o_ref[...] = acc.astype(o_ref.dtype)


def _matmul(a, b, *, out_dtype, tm, tn, name="matmul"):
    m, k = a.shape
    n = b.shape[1]
    return pl.pallas_call(
        functools.partial(_mm_kernel, has_resid=False),
        grid=(m // tm, n // tn),
        in_specs=[pl.BlockSpec((tm, k), lambda i, j: (i, 0)),
                  pl.BlockSpec((k, tn), lambda i, j: (0, j))],
        out_specs=pl.BlockSpec((tm, tn), lambda i, j: (i, j)),
        out_shape=jax.ShapeDtypeStruct((m, n), out_dtype),
        compiler_params=_params("parallel", "arbitrary"),
        name=name,
    )(a, b)


def _matmul_resid_parts(a, b, parts, *, tm, tn, name):
    m, k = a.shape
    n = b.shape[1]
    out = None
    for resid, row0 in parts:
        rb0 = row0 // tm
        extra_specs, extra_args, _ = _into(out)
        out = pl.pallas_call(
            functools.partial(_mm_kernel, has_resid=True),
            grid=(resid.shape[0] // tm, n // tn),
            in_specs=[pl.BlockSpec((tm, k), lambda i, j, rb0=rb0: (i + rb0, 0)),
                      pl.BlockSpec((k, tn), lambda i, j: (0, j)),
                      pl.BlockSpec((tm, tn), lambda i, j: (i, j))] + extra_specs,
            out_specs=pl.BlockSpec((tm, tn), lambda i, j, rb0=rb0: (i + rb0, j)),
            out_shape=jax.ShapeDtypeStruct((m, n), F32),
            input_output_aliases={3: 0} if out is not None else {},
            compiler_params=_params("parallel", "arbitrary"),
            name=name,
        )(a, b, resid, *extra_args)
    return out


def _pool_kernel(up_ref, u_ref, un_ref, w_ref, s_ref, o_ref, *, lay, halo):
    g = pl.program_id(0)
    i = pl.program_id(1)
    tm = lay.tm
    length, pos0 = lay.seq_pos(i)
    x = u_ref[...]
    prev = jnp.where(pos0 == 0, 0.0, up_ref[...])
    nxt = jnp.where(pos0 + tm == length, 0.0, un_ref[...])
    e = jnp.concatenate([prev, x, nxt], axis=0)
    n = tm + 2 * halo
    s1 = e + pltpu.roll(e, 1, 0)
    t = pos0 + lax.broadcasted_iota(jnp.int32, (tm, 1), 0)

    for gi, w in enumerate(POOL_WINDOWS):
        h = w // 2

        @pl.when(g == gi)
        def _(h=h):
            s, k = s1, 1
            while k < h:
                s = pltpu.roll(s, k, 0) + pltpu.roll(s, n - k, 0)
                k *= 2
            win = s[halo:halo + tm]
            cnt = (jnp.minimum(t + h, length) - jnp.maximum(t - h, 0)).astype(F32)
            pooled = win / cnt - x
            y = jnp.dot(pooled.astype(BF16), w_ref[...], preferred_element_type=F32)
            o_ref[...] = (y * s_ref[...]).astype(o_ref.dtype)


def _pool_mixer(u, pool_w, pool_scale, lay):
    n_groups, c, _ = pool_w.shape
    tm, halo = lay.tm, SUBLANES_F32
    assert max(POOL_WINDOWS) // 2 <= halo and len(POOL_WINDOWS) == n_groups
    hb = tm // halo
    last_hb = lay.rows // halo - 1
    return pl.pallas_call(
        functools.partial(_pool_kernel, lay=lay, halo=halo),
        grid=(n_groups, lay.tiles),
        in_specs=[
            pl.BlockSpec((halo, c), lambda g, i: (jnp.maximum(i * hb - 1, 0), g)),
            pl.BlockSpec((tm, c), lambda g, i: (i, g)),
            pl.BlockSpec((halo, c), lambda g, i: (jnp.minimum((i + 1) * hb, last_hb), g)),
            pl.BlockSpec((None, c, c), lambda g, i: (g, 0, 0)),
            pl.BlockSpec((1, c), lambda g, i: (0, g)),
        ],
        out_specs=pl.BlockSpec((tm, c), lambda g, i: (i, g)),
        out_shape=jax.ShapeDtypeStruct((lay.rows, n_groups * c), BF16),
        compiler_params=_params("parallel", "parallel"),
        name="pool_mixer",
    )(u, u, u, pool_w, pool_scale.reshape(1, n_groups * c).astype(F32))


def _chan_dft_kernel(u_ref, w_ref, a_ref, b_ref, *, n):
    ab = jnp.dot(u_ref[...].astype(BF16), w_ref[...], preferred_element_type=F32)
    a_ref[...] = ab[:, :n].astype(a_ref.dtype)
    b_ref[...] = ab[:, n:].astype(b_ref.dtype)


def _chan_dft(u, col0, heads, lay):
    n = FOURIER_HEAD_DIM
    jk = np.outer(np.arange(n), np.arange(n)) % n
    ang = 2.0 * np.pi * jk / n
    w = jnp.asarray(np.concatenate([np.cos(ang), np.sin(ang)], axis=1) / math.sqrt(n), dtype=BF16)
    cb0 = col0 // n
    out = jax.ShapeDtypeStruct((lay.rows, heads * n), BF16)
    return pl.pallas_call(
        functools.partial(_chan_dft_kernel, n=n),
        grid=(heads, lay.tiles),
        in_specs=[pl.BlockSpec((lay.tm, n), lambda h, i: (i, cb0 + h)),
                  pl.BlockSpec((n, 2 * n), lambda h, i: (0, 0))],
        out_specs=[pl.BlockSpec((lay.tm, n), lambda h, i: (i, h))] * 2,
        out_shape=[out, out],
        compiler_params=_params("parallel", "parallel"),
        name="chan_dft",
    )(u, w)


def _trig_rows_kernel(c_ref, s_ref, *, length, mult, tr, tc):
    r = (lax.broadcasted_iota(jnp.int32, (tr, tc), 0) + pl.program_id(0) * tr) * mult
    c = lax.broadcasted_iota(jnp.int32, (tr, tc), 1) + pl.program_id(1) * tc
    ang = ((r * c) & (length - 1)).astype(F32) * (2.0 * math.pi / length)
    c_ref[...] = jnp.cos(ang)
    s_ref[...] = jnp.sin(ang)


def _trig_rows(length, n_rows, mult):
    tr, tc = SUBLANES_F32, min(length, 2048)
    out = jax.ShapeDtypeStruct((n_rows, length), F32)
    return pl.pallas_call(
        functools.partial(_trig_rows_kernel, length=length, mult=mult, tr=tr, tc=tc),
        grid=(n_rows // tr, length // tc),
        out_specs=[pl.BlockSpec((tr, tc), lambda i, j: (i, j))] * 2,
        out_shape=[out, out],
        compiler_params=_params("parallel", "parallel"),
        name="trig_rows",
    )()


def _dft_mat_kernel(ca_ref, sa_ref, cb_ref, sb_ref, c_ref, s_ref, *, split, groups):
    cb, sb = cb_ref[...], sb_ref[...]
    for q in range(groups):
        ca, sa = ca_ref[q:q + 1, :], sa_ref[q:q + 1, :]
        rows = slice(q * split, (q + 1) * split)
        c_ref[rows, :] = (ca * cb - sa * sb).astype(c_ref.dtype)
        s_ref[rows, :] = (-(sa * cb + ca * sb)).astype(s_ref.dtype)


def _dft_matrices(length):
    assert length & (length - 1) == 0
    split = min(DFT_ROW_SPLIT, length // SUBLANES_F32)
    groups = SUBLANES_F32
    ca, sa = _trig_rows(length, length // split, split)
    cb, sb = _trig_rows(length, split, 1)
    tc = min(length, 2048)
    out = jax.ShapeDtypeStruct((length, length), BF16)
    return pl.pallas_call(
        functools.partial(_dft_mat_kernel, split=split, groups=groups),
        grid=(length // (split * groups), length // tc),
        in_specs=[pl.BlockSpec((groups, tc), lambda i, j: (i, j))] * 2
        + [pl.BlockSpec((split, tc), lambda i, j: (0, j))] * 2,
        out_specs=[pl.BlockSpec((split * groups, tc), lambda i, j: (i, j))] * 2,
        out_shape=[out, out],
        compiler_params=_params("parallel", "parallel"),
        name="dft_matrices",
    )(ca, sa, cb, sb)


def _seq_dft_kernel(c_ref, s_ref, a_ref, b_ref, *rest, scale):
    o_ref, acc_ref = rest[-2], rest[-1]
    k = pl.program_id(2)

    @pl.when(k == 0)
    def _():
        acc_ref[...] = jnp.zeros_like(acc_ref)

    acc_ref[...] += (jnp.dot(c_ref[...], a_ref[...], preferred_element_type=F32)
                     + jnp.dot(s_ref[...], b_ref[...], preferred_element_type=F32))

    @pl.when(k == pl.num_programs(2) - 1)
    def _():
        o_ref[...] = (acc_ref[...] * scale).astype(o_ref.dtype)


def _seq_dft(va, vb, row0, n_seq, length, tile, dst=None):
    cmat, nsmat = _dft_matrices(length)
    total, n = va.shape
    tm = tk = min(tile, length)
    nt = length // tm
    rb0 = row0 // tk
    extra_specs, extra_args, _ = _into(dst)
    return pl.pallas_call(
        functools.partial(_seq_dft_kernel, scale=1.0 / math.sqrt(length)),
        grid=(n_seq, nt, nt),
        in_specs=[pl.BlockSpec((tm, tk), lambda s, i, k: (i, k)),
                  pl.BlockSpec((tm, tk), lambda s, i, k: (i, k)),
                  pl.BlockSpec((tk, n), lambda s, i, k: (rb0 + s * nt + k, 0)),
                  pl.BlockSpec((tk, n), lambda s, i, k: (rb0 + s * nt + k, 0))] + extra_specs,
        out_specs=pl.BlockSpec((tm, n), lambda s, i, k: (rb0 + s * nt + i, 0)),
        out_shape=jax.ShapeDtypeStruct((total, n), BF16),
        scratch_shapes=[pltpu.VMEM((tm, n), F32)],
        input_output_aliases={4: 0} if dst is not None else {},
        compiler_params=_params("parallel", "parallel", "arbitrary"),
        name="seq_dft",
    )(cmat, nsmat, va, vb, *extra_args)


def _out_proj2_kernel(a_ref, b_ref, w1_ref, w2_ref, r_ref, *rest):
    o_ref = rest[-1]
    acc = jnp.dot(a_ref[...], w1_ref[...], preferred_element_type=F32)
    acc = acc + jnp.dot(b_ref[...], w2_ref[...], preferred_element_type=F32)
    o_ref[...] = acc + r_ref[...]


def _out_proj2_parts(a, b, w, parts, *, tm, tn):
    m, ka = a.shape
    kb = b.shape[1]
    n = w.shape[1]
    assert ka % kb == 0
    out = None
    for resid, row0 in parts:
        rb0 = row0 // tm
        extra_specs, extra_args, _ = _into(out)
        out = pl.pallas_call(
            _out_proj2_kernel,
            grid=(resid.shape[0] // tm, n // tn),
            in_specs=[pl.BlockSpec((tm, ka), lambda i, j, rb0=rb0: (i + rb0, 0)),
                      pl.BlockSpec((tm, kb), lambda i, j, rb0=rb0: (i + rb0, 0)),
                      pl.BlockSpec((ka, tn), lambda i, j: (0, j)),
                      pl.BlockSpec((kb, tn), lambda i, j: (ka // kb, j)),
                      pl.BlockSpec((tm, tn), lambda i, j: (i, j))] + extra_specs,
            out_specs=pl.BlockSpec((tm, tn), lambda i, j, rb0=rb0: (i + rb0, j)),
            out_shape=jax.ShapeDtypeStruct((m, n), F32),
            input_output_aliases={5: 0} if out is not None else {},
            compiler_params=_params("parallel", "arbitrary"),
            name="even_out_proj",
        )(a, b, w, w, resid, *extra_args)
    return out


def _rope_tables_kernel(inv_ref, cos_ref, sa_ref, sb_ref, *, tr, half, length):
    row = lax.broadcasted_iota(jnp.int32, (tr, LANES), 0) + pl.program_id(0) * tr
    lane = lax.broadcasted_iota(jnp.int32, (tr, LANES), 1)
    ang = row.astype(F32) * inv_ref[...]
    c, s = jnp.cos(ang), jnp.sin(ang)
    rot = row < length
    cos_ref[...] = jnp.where(rot & (lane < 2 * half), c, 1.0)
    sa_ref[...] = jnp.where(rot & (lane >= half) & (lane < 2 * half), s, 0.0)
    sb_ref[...] = jnp.where(rot & (lane < half), -s, 0.0)


def _rope_tables(length, head_dim, extra_rows):
    assert head_dim == LANES
    rot = head_dim // 4
    half = rot // 2
    inv = np.ones((1, LANES), np.float64)
    freqs = ROPE_THETA ** (-np.arange(0, rot, 2, dtype=np.float64) / rot)
    inv[0, :half] = freqs
    inv[0, half:rot] = freqs
    tr = min(length, 512)
    assert extra_rows % tr == 0
    out = jax.ShapeDtypeStruct((length + extra_rows, LANES), F32)
    return pl.pallas_call(
        functools.partial(_rope_tables_kernel, tr=tr, half=half, length=length),
        grid=((length + extra_rows) // tr,),
        in_specs=[pl.BlockSpec((1, LANES), lambda i: (0, 0))],
        out_specs=[pl.BlockSpec((tr, LANES), lambda i: (i, 0))] * 3,
        out_shape=[out, out, out],
        compiler_params=_params("parallel"),
        name="rope_tables",
    )(jnp.asarray(inv, dtype=F32))


def _qkv_kernel(h_ref, w_ref, cos_ref, sa_ref, sb_ref, o_ref, *, half):
    y = jnp.dot(h_ref[...], w_ref[...], preferred_element_type=F32)
    cos, sa, sb = cos_ref[...], sa_ref[...], sb_ref[...]
    for c in range(y.shape[1] // LANES):
        yc = y[:, c * LANES:(c + 1) * LANES]
        out = yc * cos + pltpu.roll(yc, half, 1) * sa + pltpu.roll(yc, LANES - half, 1) * sb
        o_ref[:, c * LANES:(c + 1) * LANES] = out.astype(o_ref.dtype)


def _qkv_proj(h, w, lay, head_dim, *, tn):
    m, k = h.shape
    n = w.shape[1]
    tm = lay.tm
    max_len = max(lay.lp, lay.ls)
    tables = _rope_tables(max_len, head_dim, tm)
    rope_tiles = (2 * n // 3) // tn

    def pos_block(i, j):
        in_p = i < lay.tiles_p
        blk = jnp.where(in_p, i % (lay.lp // tm), (i - lay.tiles_p) % (lay.ls // tm))
        return jnp.where(j < rope_tiles, blk, max_len // tm), 0

    tab = pl.BlockSpec((tm, LANES), pos_block)
    return pl.pallas_call(
        functools.partial(_qkv_kernel, half=head_dim // 8),
        grid=(m // tm, n // tn),
        in_specs=[pl.BlockSpec((tm, k), lambda i, j: (i, 0)),
                  pl.BlockSpec((k, tn), lambda i, j: (0, j)),
                  tab, tab, tab],
        out_specs=pl.BlockSpec((tm, tn), lambda i, j: (i, j)),
        out_shape=jax.ShapeDtypeStruct((m, n), BF16),
        compiler_params=_params("parallel", "arbitrary"),
        name="qkv_proj",
    )(h, w, *tables)


def _attn_kernel(lq1_ref, lk1_ref, lq2_ref, lk2_ref, q_ref, k_ref, v_ref, w_ref, *rest, hd, scale, lam_init, tk):
    o_ref = rest[-1]
    lam = (jnp.exp(jnp.sum(lq1_ref[...] * lk1_ref[...], keepdims=True))
           - jnp.exp(jnp.sum(lq2_ref[...] * lk2_ref[...], keepdims=True)) + lam_init)
    q = q_ref[...]
    qm = (q[:, :hd], q[:, hd:])
    dn = (((1,), (1,)), ((), ()))
    c2 = scale * LOG2_E
    m, l, acc = [None, None], [None, None], [None, None]
    for c in range(k_ref.shape[0] // tk):
        kc = k_ref[c * tk:(c + 1) * tk, :]
        vc = v_ref[c * tk:(c + 1) * tk, :]
        for mp in range(2):
            s = lax.dot_general(qm[mp], kc[:, mp * hd:(mp + 1) * hd], dn, preferred_element_type=F32) * c2
            mc = jnp.max(s, axis=-1, keepdims=True)
            m_new = mc if c == 0 else jnp.maximum(m[mp], mc)
            p = jnp.exp2(s - m_new)
            ls = jnp.sum(p, axis=-1, keepdims=True)
            pv = jnp.dot(p.astype(BF16), vc, preferred_element_type=F32)
            if c == 0:
                l[mp], acc[mp] = ls, pv
            else:
                alpha = jnp.exp2(m[mp] - m_new)
                l[mp] = alpha * l[mp] + ls
                acc[mp] = alpha * acc[mp] + pv
            m[mp] = m_new
    o = acc[0] * (1.0 / l[0]) - acc[1] * (lam / l[1])
    o = o * lax.rsqrt(jnp.mean(o * o, axis=-1, keepdims=True) + SUBLN_EPS)
    o_ref[...] = (o * w_ref[...] * (1.0 - lam_init)).astype(o_ref.dtype)


def _diff_attention(qkv, lams, subln_w, row0, n_seq, length, d_model, hd, layer, *, tq, dst=None):
    heads = d_model // (2 * hd)
    lam_init = 0.8 - 0.6 * math.exp(-LAMBDA_INIT_DECAY * layer)
    qt = length // tq
    qb0, sb0 = row0 // tq, row0 // length
    vec = pl.BlockSpec((1, hd), lambda b, h, i: (0, 0))
    kv_mode = pl.Buffered(1)
    extra_specs, extra_args, _ = _into(dst)
    return pl.pallas_call(
        functools.partial(_attn_kernel, hd=hd, scale=hd ** -0.5, lam_init=lam_init, tk=min(ATTN_KV_CHUNK, length)),
        grid=(n_seq, heads, qt),
        in_specs=[vec, vec, vec, vec,
                  pl.BlockSpec((tq, 2 * hd), lambda b, h, i: (qb0 + b * qt + i, h)),
                  pl.BlockSpec((length, 2 * hd), lambda b, h, i: (sb0 + b, heads + h), pipeline_mode=kv_mode),
                  pl.BlockSpec((length, 2 * hd), lambda b, h, i: (sb0 + b, 2 * heads + h), pipeline_mode=kv_mode),
                  pl.BlockSpec((1, 2 * hd), lambda b, h, i: (0, 0))] + extra_specs,
        out_specs=pl.BlockSpec((tq, 2 * hd), lambda b, h, i: (qb0 + b * qt + i, h)),
        out_shape=jax.ShapeDtypeStruct((qkv.shape[0], d_model), BF16),
        input_output_aliases={8: 0} if dst is not None else {},
        compiler_params=_params("parallel", "parallel", "arbitrary"),
        name="diff_attention",
    )(*[v.reshape(1, hd).astype(F32) for v in lams], qkv, qkv, qkv, subln_w.reshape(1, 2 * hd).astype(F32),
      *extra_args)


def _ffn_up_kernel(hp_ref, h_ref, hn_ref, wg_ref, wu0_ref, wu1_ref, cw_ref, cb_ref, g_ref, lhs_ref, *,
                   lay, halo, tn, f):
    i, j = pl.program_id(0), pl.program_id(1)
    tm = lay.tm

    @pl.when(j == 0)
    def _():
        length, pos0 = lay.seq_pos(i)
        zero = jnp.zeros(hp_ref.shape, hp_ref.dtype)
        lhs_ref[0:halo, :] = jnp.where(pos0 == 0, zero, hp_ref[...])
        lhs_ref[halo:halo + tm, :] = h_ref[...]
        lhs_ref[halo + tm:, :] = jnp.where(pos0 + tm == length, zero, hn_ref[...])

    n = tm + 2 * halo
    lhs = lhs_ref[...]
    cw, cb = cw_ref[...], cb_ref[...]

    def conv(u, c0):
        k = slice(c0, c0 + tn)
        return (pltpu.roll(u, 1, 0)[halo:halo + tm] * cw[0:1, k] + u[halo:halo + tm] * cw[1:2, k]
                + pltpu.roll(u, n - 1, 0)[halo:halo + tm] * cw[2:3, k] + cb[:, k])

    gate = conv(jnp.dot(lhs, wg_ref[...], preferred_element_type=F32), 0)
    up = conv(jnp.concatenate([jnp.dot(lhs, wu0_ref[...], preferred_element_type=F32),
                               jnp.dot(lhs, wu1_ref[...], preferred_element_type=F32)], axis=1), tn)
    g = gate / (1.0 + jnp.exp(-gate)) * up
    col = j * tn + lax.broadcasted_iota(jnp.int32, (1, tn), 1)
    g_ref[...] = jnp.where(col < f, g, 0.0).astype(g_ref.dtype)


def _gate_up_tiles(x, f, f_pad, tn):
    r = x.shape[0]
    x = jnp.pad(x.reshape(r, 2, f), ((0, 0), (0, 0), (0, f_pad - f)))
    return jnp.swapaxes(x.reshape(r, 2, f_pad // tn, tn), 1, 2).reshape(r, 2 * f_pad)


def _ffn_up(h, w_up, layer, conv_w, conv_b, lay, *, tn, f_pad):
    m, k = h.shape
    f = w_up.shape[2] // 2
    half = tn // 2
    assert f % half == 0
    tm, halo = lay.tm, SUBLANES_BF16
    hb = tm // halo
    last_hb = m // halo - 1
    up0, last_half = f // half, 2 * f // half - 1
    return pl.pallas_call(
        functools.partial(_ffn_up_kernel, lay=lay, halo=halo, tn=tn, f=f),
        grid=(m // tm, f_pad // tn),
        in_specs=[pl.BlockSpec((halo, k), lambda i, j: (jnp.maximum(i * hb - 1, 0), 0)),
                  pl.BlockSpec((tm, k), lambda i, j: (i, 0)),
                  pl.BlockSpec((halo, k), lambda i, j: (jnp.minimum((i + 1) * hb, last_hb), 0)),
                  pl.BlockSpec((None, k, tn), lambda i, j: (layer, 0, j)),
                  pl.BlockSpec((None, k, half), lambda i, j: (layer, 0, up0 + 2 * j)),
                  pl.BlockSpec((None, k, half), lambda i, j: (layer, 0, jnp.minimum(up0 + 2 * j + 1, last_half))),
                  pl.BlockSpec((3, 2 * tn), lambda i, j: (0, j)),
                  pl.BlockSpec((1, 2 * tn), lambda i, j: (0, j))],
        out_specs=pl.BlockSpec((tm, tn), lambda i, j: (i, j)),
        out_shape=jax.ShapeDtypeStruct((m, f_pad), BF16),
        scratch_shapes=[pltpu.VMEM((tm + 2 * halo, k), BF16)],
        compiler_params=_params("parallel", "arbitrary"),
        name="ffn_up",
    )(h, h, h, w_up, w_up, w_up,
      _gate_up_tiles(conv_w.astype(F32), f, f_pad, tn), _gate_up_tiles(conv_b.astype(F32).reshape(1, 2 * f), f, f_pad, tn))


def _ffn_down_kernel(g_ref, w_ref, r_ref, o_ref):
    @pl.when(pl.program_id(2) == 0)
    def _():
        o_ref[...] = r_ref[...]

    o_ref[...] += jnp.dot(g_ref[...], w_ref[...], preferred_element_type=F32)


def _ffn_down(g, w, layer, resid, *, tm, tn, tk):
    m, k = g.shape
    n = w.shape[2]
    return pl.pallas_call(
        _ffn_down_kernel,
        grid=(m // tm, n // tn, k // tk),
        in_specs=[pl.BlockSpec((tm, tk), lambda i, j, kk: (i, kk)),
                  pl.BlockSpec((None, tk, tn), lambda i, j, kk: (layer, kk, j)),
                  pl.BlockSpec((tm, tn), lambda i, j, kk: (i, j))],
        out_specs=pl.BlockSpec((tm, tn), lambda i, j, kk: (i, j)),
        out_shape=jax.ShapeDtypeStruct((m, n), F32),
        compiler_params=_params("parallel", "parallel", "arbitrary"),
        name="ffn_down",
    )(g, w, resid)


def _down_k_tile(f_pad):
    units = f_pad // MXU_DIM
    best = 1
    for d in range(1, units + 1):
        if units % d == 0 and d * MXU_DIM * 4 <= max(f_pad, 4 * MXU_DIM):
            best = d
    return best * MXU_DIM


def kernel(x_prompt, x_sample, mix_norm_even, w_in_even, pool_w, pool_scale, fourier_w, w_out_even, mix_norm_odd,
           w_qkv, lambda_q1, lambda_k1, lambda_q2, lambda_k2, subln_w, w_out_odd, ffn_norm, w_up, conv_w, conv_b,
           w_down, final_norm):
    bp, lp, d = x_prompt.shape
    bs, ls, _ = x_sample.shape
    assert bp == 1, "the prompt group is handled as one sequence"
    lay = _Layout(bp, lp, bs, ls)
    tm = lay.tm
    depth = ffn_norm.shape[0]
    f = w_down.shape[1]
    tn_ffn = FFN_COL_TILE
    f_pad = -(-f // tn_ffn) * tn_ffn
    hd = lambda_q1.shape[1]
    pool_dim = pool_w.shape[1] * pool_w.shape[2]
    fourier_dim = fourier_w.shape[1]
    col_tile = min(512, d)

    w_up_b = w_up.astype(BF16)
    w_down_b = jnp.pad(w_down, ((0, 0), (0, f_pad - f), (0, 0))).astype(BF16)

    parts = [(x_prompt.reshape(lp, d), 0), (x_sample.reshape(bs * ls, d), lay.rows_p)]

    for layer in range(depth):
        i = layer // 2
        if layer % 2 == 0:
            h = _rmsnorm_parts(parts, mix_norm_even[i], lay.rows)
            u = _matmul(h, w_in_even[i].astype(BF16), out_dtype=F32, tm=tm, tn=col_tile, name="even_in_proj")
            a = _pool_mixer(u, pool_w[i].astype(BF16), pool_scale[i], lay)
            va, vb = _chan_dft(u, pool_dim, fourier_dim // FOURIER_HEAD_DIM, lay)
            fr = _seq_dft(va, vb, 0, bp, lp, MAX_ROW_TILE)
            fr = _seq_dft(va, vb, lay.rows_p, bs, ls, MAX_ROW_TILE, dst=fr)
            b = _matmul(fr, fourier_w[i].astype(BF16), out_dtype=BF16, tm=tm, tn=col_tile, name="fourier_proj")
            x = _out_proj2_parts(a, b, w_out_even[i].astype(BF16), parts, tm=tm, tn=col_tile)
        else:
            h = _rmsnorm_parts(parts, mix_norm_odd[i], lay.rows)
            qkv = _qkv_proj(h, w_qkv[i].astype(BF16), lay, hd, tn=min(QKV_COL_TILE, d))
            lams = (lambda_q1[i], lambda_k1[i], lambda_q2[i], lambda_k2[i])
            o = _diff_attention(qkv, lams, subln_w[i], 0, bp, lp, d, hd, layer, tq=min(256, lp))
            o = _diff_attention(qkv, lams, subln_w[i], lay.rows_p, bs, ls, d, hd, layer, tq=min(512, ls), dst=o)
            x = _matmul_resid_parts(o, w_out_odd[i].astype(BF16), parts, tm=tm, tn=col_tile, name="odd_out_proj")
        parts = [(x, 0)]

        h = _rmsnorm_parts(parts, ffn_norm[layer], lay.rows)
        g = _ffn_up(h, w_up_b, layer, conv_w[layer], conv_b[layer], lay, tn=tn_ffn, f_pad=f_pad)
        x = _ffn_down(g, w_down_b, layer, x, tm=tm, tn=min(1024, d), tk=_down_k_tile(f_pad))
        parts = [(x, 0)]

    y_prompt = _rmsnorm(x, final_norm, out_dtype=F32, in_row0=0, rows=lay.rows_p)
    y_sample = _rmsnorm(x, final_norm, out_dtype=F32, in_row0=lay.rows_p, rows=lay.rows_s)
    return y_prompt.reshape(bp, lp, d), y_sample.reshape(bs, ls, d)
```

```python
import functools
import math

import numpy as np
import jax
import jax.numpy as jnp
from jax import lax
from jax.experimental import pallas as pl
from jax.experimental.pallas import tpu as pltpu

F32 = jnp.float32
BF16 = jnp.bfloat16

POOL_WINDOWS = (2, 4, 8, 16)
FOURIER_HEAD_DIM = 256
ROPE_THETA = 500000.0
NORM_EPS = 1e-6
SUBLN_EPS = 1e-5
LAMBDA_INIT_DECAY = 0.3

LANES = 128
SUBLANES_F32 = 8
SUBLANES_BF16 = 16
MXU_DIM = 256
VMEM_LIMIT_BYTES = 56 * 1024 * 1024

MAX_ROW_TILE = 1024
NORM_ROW_TILE = 256
FFN_COL_TILE = 512
QKV_COL_TILE = 1024
ATTN_KV_CHUNK = 512
ATTN_Q_TILE_LONG = 512
ATTN_Q_TILE_SHORT = 1024
DFT_ROW_SPLIT = 64
LOG2_E = math.log2(math.e)


def _params(*sem):
    return pltpu.CompilerParams(dimension_semantics=sem, vmem_limit_bytes=VMEM_LIMIT_BYTES)


class _Layout:
    def __init__(self, n_prompt, len_prompt, n_sample, len_sample):
        self.lp, self.ls = len_prompt, len_sample
        self.rows_p, self.rows_s = n_prompt * len_prompt, n_sample * len_sample
        self.rows = self.rows_p + self.rows_s
        self.tm = min(MAX_ROW_TILE, len_prompt, len_sample)
        assert len_prompt % self.tm == 0 and len_sample % self.tm == 0
        self.tiles_p = self.rows_p // self.tm
        self.tiles = self.rows // self.tm

    def seq_pos(self, i):
        in_p = i < self.tiles_p
        length = jnp.where(in_p, self.lp, self.ls)
        tile_in_seq = jnp.where(in_p, i % (self.lp // self.tm), (i - self.tiles_p) % (self.ls // self.tm))
        return length, tile_in_seq * self.tm


def _into(dst):
    if dst is None:
        return [], [], None
    return [pl.BlockSpec(memory_space=pl.ANY)], [dst], dst


def _lagged_tiles(n_i, n_j):
    n = n_i * n_j

    def cur(s):
        c = jnp.minimum(s, n - 1)
        return c // n_j, c % n_j

    def prev(s):
        p = jnp.maximum(s - 1, 0)
        return p // n_j, p % n_j

    return n, cur, prev


def _rmsnorm_kernel(x_ref, g_ref, *rest, eps):
    o_ref = rest[-1]
    x = x_ref[...]
    y = x * lax.rsqrt(jnp.mean(x * x, axis=-1, keepdims=True) + eps)
    o_ref[...] = (y * g_ref[...]).astype(o_ref.dtype)


def _rmsnorm(x, g, *, out_dtype, in_row0=0, rows=None, out_rows=None, out_row0=0, dst=None):
    d = x.shape[1]
    tm = NORM_ROW_TILE
    rows = x.shape[0] if rows is None else rows
    out_rows = rows if out_rows is None else out_rows
    ib0, ob0 = in_row0 // tm, out_row0 // tm
    extra_specs, extra_args, _ = _into(dst)
    return pl.pallas_call(
        functools.partial(_rmsnorm_kernel, eps=NORM_EPS),
        grid=(rows // tm,),
        in_specs=[pl.BlockSpec((tm, d), lambda i: (i + ib0, 0)),
                  pl.BlockSpec((1, d), lambda i: (0, 0))] + extra_specs,
        out_specs=pl.BlockSpec((tm, d), lambda i: (i + ob0, 0)),
        out_shape=jax.ShapeDtypeStruct((out_rows, d), out_dtype),
        input_output_aliases={2: 0} if dst is not None else {},
        compiler_params=_params("parallel"),
        name="rmsnorm",
    )(x, g.reshape(1, d).astype(F32), *extra_args)


def _rmsnorm_parts(parts, g, total_rows):
    h = None
    for arr, row0 in parts:
        h = _rmsnorm(arr, g, out_dtype=BF16, out_rows=total_rows, out_row0=row0, dst=h)
    return h


def _mm_kernel(a_ref, b_ref, *rest, has_resid):
    o_ref = rest[-1]
    acc = jnp.dot(a_ref[...], b_ref[...], preferred_element_type=F32)
    if has_resid:
        acc = acc + rest[0][...]
    o_ref[...] = acc.astype(o_ref.dtype)


def _matmul(a, b, *, out_dtype, tm, tn, name="matmul"):
    m, k = a.shape
    n = b.shape[1]
    return pl.pallas_call(
        functools.partial(_mm_kernel, has_resid=False),
        grid=(m // tm, n // tn),
        in_specs=[pl.BlockSpec((tm, k), lambda i, j: (i, 0)),
                  pl.BlockSpec((k, tn), lambda i, j: (0, j))],
        out_specs=pl.BlockSpec((tm, tn), lambda i, j: (i, j)),
        out_shape=jax.ShapeDtypeStruct((m, n), out_dtype),
        compiler_params=_params("parallel", "arbitrary"),
        name=name,
    )(a, b)


def _emit_stream(x, j, o_ref, xb_ref, ss_ref):
    o_ref[...] = x
    xb_ref[...] = x.astype(xb_ref.dtype)
    part = jnp.broadcast_to(jnp.sum(x * x, axis=-1, keepdims=True), ss_ref.shape)

    @pl.when(j == 0)
    def _():
        ss_ref[...] = part

    @pl.when(j > 0)
    def _():
        ss_ref[...] += part


def _stream_outs(m, n, tm, tn, row_block):
    specs = [pl.BlockSpec((tm, tn), lambda i, j: (row_block(i), j)),
             pl.BlockSpec((tm, tn), lambda i, j: (row_block(i), j)),
             pl.BlockSpec((tm, LANES), lambda i, j: (row_block(i), 0))]
    shapes = [jax.ShapeDtypeStruct((m, n), F32), jax.ShapeDtypeStruct((m, n), BF16),
              jax.ShapeDtypeStruct((m, LANES), F32)]
    return specs, shapes


def _into3(dsts, first_index):
    if dsts is None:
        return [], [], {}
    return ([pl.BlockSpec(memory_space=pl.ANY)] * 3, list(dsts), {first_index + t: t for t in range(3)})


def _mm_resid_stream_kernel(a_ref, b_ref, r_ref, *rest):
    o_ref, xb_ref, ss_ref = rest[-3:]
    x = jnp.dot(a_ref[...], b_ref[...], preferred_element_type=F32) + r_ref[...]
    _emit_stream(x, pl.program_id(1), o_ref, xb_ref, ss_ref)


def _matmul_resid_parts(a, b, parts, *, tm, tn, name):
    m, k = a.shape
    n = b.shape[1]
    outs = None
    for resid, row0 in parts:
        rb0 = row0 // tm
        extra_specs, extra_args, aliases = _into3(outs, 3)
        out_specs, out_shape = _stream_outs(m, n, tm, tn, lambda i, rb0=rb0: i + rb0)
        outs = pl.pallas_call(
            _mm_resid_stream_kernel,
            grid=(resid.shape[0] // tm, n // tn),
            in_specs=[pl.BlockSpec((tm, k), lambda i, j, rb0=rb0: (i + rb0, 0)),
                      pl.BlockSpec((k, tn), lambda i, j: (0, j)),
                      pl.BlockSpec((tm, tn), lambda i, j: (i, j))] + extra_specs,
            out_specs=out_specs,
            out_shape=out_shape,
            input_output_aliases=aliases,
            compiler_params=_params("parallel", "arbitrary"),
            name=name,
        )(a, b, resid, *extra_args)
    return outs


def _pool_kernel(up_ref, u_ref, un_ref, w_ref, s_ref, o_ref, *, lay, halo):
    g = pl.program_id(0)
    i = pl.program_id(1)
    tm = lay.tm
    length, pos0 = lay.seq_pos(i)
    x = u_ref[...]
    prev = jnp.where(pos0 == 0, 0.0, up_ref[...])
    nxt = jnp.where(pos0 + tm == length, 0.0, un_ref[...])
    e = jnp.concatenate([prev, x, nxt], axis=0)
    n = tm + 2 * halo
    s1 = e + pltpu.roll(e, 1, 0)
    t = pos0 + lax.broadcasted_iota(jnp.int32, (tm, 1), 0)

    for gi, w in enumerate(POOL_WINDOWS):
        h = w // 2

        @pl.when(g == gi)
        def _(h=h):
            s, k = s1, 1
            while k < h:
                s = pltpu.roll(s, k, 0) + pltpu.roll(s, n - k, 0)
                k *= 2
            win = s[halo:halo + tm]
            cnt = (jnp.minimum(t + h, length) - jnp.maximum(t - h, 0)).astype(F32)
            pooled = win / cnt - x
            y = jnp.dot(pooled.astype(BF16), w_ref[...], preferred_element_type=F32)
            o_ref[...] = (y * s_ref[...]).astype(o_ref.dtype)


def _pool_mixer(u, pool_w, pool_scale, lay):
    n_groups, c, _ = pool_w.shape
    tm, halo = lay.tm, SUBLANES_F32
    assert max(POOL_WINDOWS) // 2 <= halo and len(POOL_WINDOWS) == n_groups
    hb = tm // halo
    last_hb = lay.rows // halo - 1
    return pl.pallas_call(
        functools.partial(_pool_kernel, lay=lay, halo=halo),
        grid=(n_groups, lay.tiles),
        in_specs=[
            pl.BlockSpec((halo, c), lambda g, i: (jnp.maximum(i * hb - 1, 0), g)),
            pl.BlockSpec((tm, c), lambda g, i: (i, g)),
            pl.BlockSpec((halo, c), lambda g, i: (jnp.minimum((i + 1) * hb, last_hb), g)),
            pl.BlockSpec((None, c, c), lambda g, i: (g, 0, 0)),
            pl.BlockSpec((1, c), lambda g, i: (0, g)),
        ],
        out_specs=pl.BlockSpec((tm, c), lambda g, i: (i, g)),
        out_shape=jax.ShapeDtypeStruct((lay.rows, n_groups * c), BF16),
        compiler_params=_params("parallel", "parallel"),
        name="pool_mixer",
    )(u, u, u, pool_w, pool_scale.reshape(1, n_groups * c).astype(F32))


def _chan_dft_kernel(u_ref, w_ref, a_ref, b_ref, *, n):
    ab = jnp.dot(u_ref[...].astype(BF16), w_ref[...], preferred_element_type=F32)
    a_ref[...] = ab[:, :n].astype(a_ref.dtype)
    b_ref[...] = ab[:, n:].astype(b_ref.dtype)


def _chan_dft(u, col0, heads, lay):
    n = FOURIER_HEAD_DIM
    jk = np.outer(np.arange(n), np.arange(n)) % n
    ang = 2.0 * np.pi * jk / n
    w = jnp.asarray(np.concatenate([np.cos(ang), np.sin(ang)], axis=1) / math.sqrt(n), dtype=BF16)
    cb0 = col0 // n
    out = jax.ShapeDtypeStruct((lay.rows, heads * n), BF16)
    return pl.pallas_call(
        functools.partial(_chan_dft_kernel, n=n),
        grid=(heads, lay.tiles),
        in_specs=[pl.BlockSpec((lay.tm, n), lambda h, i: (i, cb0 + h)),
                  pl.BlockSpec((n, 2 * n), lambda h, i: (0, 0))],
        out_specs=[pl.BlockSpec((lay.tm, n), lambda h, i: (i, h))] * 2,
        out_shape=[out, out],
        compiler_params=_params("parallel", "parallel"),
        name="chan_dft",
    )(u, w)


def _trig_rows_kernel(c_ref, s_ref, *, length, mult, tr, tc):
    r = (lax.broadcasted_iota(jnp.int32, (tr, tc), 0) + pl.program_id(0) * tr) * mult
    c = lax.broadcasted_iota(jnp.int32, (tr, tc), 1) + pl.program_id(1) * tc
    ang = ((r * c) & (length - 1)).astype(F32) * (2.0 * math.pi / length)
    c_ref[...] = jnp.cos(ang)
    s_ref[...] = jnp.sin(ang)


def _trig_rows(length, n_rows, mult):
    tr, tc = SUBLANES_F32, min(length, 2048)
    out = jax.ShapeDtypeStruct((n_rows, length), F32)
    return pl.pallas_call(
        functools.partial(_trig_rows_kernel, length=length, mult=mult, tr=tr, tc=tc),
        grid=(n_rows // tr, length // tc),
        out_specs=[pl.BlockSpec((tr, tc), lambda i, j: (i, j))] * 2,
        out_shape=[out, out],
        compiler_params=_params("parallel", "parallel"),
        name="trig_rows",
    )()


def _dft_mat_kernel(ca_ref, sa_ref, cb_ref, sb_ref, c_ref, s_ref, *, split, groups):
    cb, sb = cb_ref[...], sb_ref[...]
    for q in range(groups):
        ca, sa = ca_ref[q:q + 1, :], sa_ref[q:q + 1, :]
        rows = slice(q * split, (q + 1) * split)
        c_ref[rows, :] = (ca * cb - sa * sb).astype(c_ref.dtype)
        s_ref[rows, :] = (-(sa * cb + ca * sb)).astype(s_ref.dtype)


def _dft_matrices(length):
    assert length & (length - 1) == 0
    split = min(DFT_ROW_SPLIT, length // SUBLANES_F32)
    groups = SUBLANES_F32
    ca, sa = _trig_rows(length, length // split, split)
    cb, sb = _trig_rows(length, split, 1)
    tc = min(length, 2048)
    out = jax.ShapeDtypeStruct((length, length), BF16)
    return pl.pallas_call(
        functools.partial(_dft_mat_kernel, split=split, groups=groups),
        grid=(length // (split * groups), length // tc),
        in_specs=[pl.BlockSpec((groups, tc), lambda i, j: (i, j))] * 2
        + [pl.BlockSpec((split, tc), lambda i, j: (0, j))] * 2,
        out_specs=[pl.BlockSpec((split * groups, tc), lambda i, j: (i, j))] * 2,
        out_shape=[out, out],
        compiler_params=_params("parallel", "parallel"),
        name="dft_matrices",
    )(ca, sa, cb, sb)


def _seq_dft_kernel(c_ref, s_ref, a_ref, b_ref, *rest, scale):
    o_ref, acc_ref = rest[-2], rest[-1]
    k = pl.program_id(2)

    @pl.when(k == 0)
    def _():
        acc_ref[...] = jnp.zeros_like(acc_ref)

    acc_ref[...] += (jnp.dot(c_ref[...], a_ref[...], preferred_element_type=F32)
                     + jnp.dot(s_ref[...], b_ref[...], preferred_element_type=F32))

    @pl.when(k == pl.num_programs(2) - 1)
    def _():
        o_ref[...] = (acc_ref[...] * scale).astype(o_ref.dtype)


def _seq_dft(va, vb, row0, n_seq, length, tile, dst=None):
    cmat, nsmat = _dft_matrices(length)
    total, n = va.shape
    tm = tk = min(tile, length)
    nt = length // tm
    rb0 = row0 // tk
    extra_specs, extra_args, _ = _into(dst)
    return pl.pallas_call(
        functools.partial(_seq_dft_kernel, scale=1.0 / math.sqrt(length)),
        grid=(n_seq, nt, nt),
        in_specs=[pl.BlockSpec((tm, tk), lambda s, i, k: (i, k)),
                  pl.BlockSpec((tm, tk), lambda s, i, k: (i, k)),
                  pl.BlockSpec((tk, n), lambda s, i, k: (rb0 + s * nt + k, 0)),
                  pl.BlockSpec((tk, n), lambda s, i, k: (rb0 + s * nt + k, 0))] + extra_specs,
        out_specs=pl.BlockSpec((tm, n), lambda s, i, k: (rb0 + s * nt + i, 0)),
        out_shape=jax.ShapeDtypeStruct((total, n), BF16),
        scratch_shapes=[pltpu.VMEM((tm, n), F32)],
        input_output_aliases={4: 0} if dst is not None else {},
        compiler_params=_params("parallel", "parallel", "arbitrary"),
        name="seq_dft",
    )(cmat, nsmat, va, vb, *extra_args)


def _out_proj2_kernel(a_ref, b_ref, w1_ref, w2_ref, r_ref, *rest):
    o_ref, xb_ref, ss_ref = rest[-3:]
    acc = jnp.dot(a_ref[...], w1_ref[...], preferred_element_type=F32)
    acc = acc + jnp.dot(b_ref[...], w2_ref[...], preferred_element_type=F32)
    _emit_stream(acc + r_ref[...], pl.program_id(1), o_ref, xb_ref, ss_ref)


def _out_proj2_parts(a, b, w, parts, *, tm, tn):
    m, ka = a.shape
    kb = b.shape[1]
    n = w.shape[1]
    assert ka % kb == 0
    outs = None
    for resid, row0 in parts:
        rb0 = row0 // tm
        extra_specs, extra_args, aliases = _into3(outs, 5)
        out_specs, out_shape = _stream_outs(m, n, tm, tn, lambda i, rb0=rb0: i + rb0)
        outs = pl.pallas_call(
            _out_proj2_kernel,
            grid=(resid.shape[0] // tm, n // tn),
            in_specs=[pl.BlockSpec((tm, ka), lambda i, j, rb0=rb0: (i + rb0, 0)),
                      pl.BlockSpec((tm, kb), lambda i, j, rb0=rb0: (i + rb0, 0)),
                      pl.BlockSpec((ka, tn), lambda i, j: (0, j)),
                      pl.BlockSpec((kb, tn), lambda i, j: (ka // kb, j)),
                      pl.BlockSpec((tm, tn), lambda i, j: (i, j))] + extra_specs,
            out_specs=out_specs,
            out_shape=out_shape,
            input_output_aliases=aliases,
            compiler_params=_params("parallel", "arbitrary"),
            name="even_out_proj",
        )(a, b, w, w, resid, *extra_args)
    return outs


def _rope_tables_kernel(inv_ref, cos_ref, sa_ref, sb_ref, *, tr, half, length):
    row = lax.broadcasted_iota(jnp.int32, (tr, LANES), 0) + pl.program_id(0) * tr
    lane = lax.broadcasted_iota(jnp.int32, (tr, LANES), 1)
    ang = row.astype(F32) * inv_ref[...]
    c, s = jnp.cos(ang), jnp.sin(ang)
    rot = row < length
    cos_ref[...] = jnp.where(rot & (lane < 2 * half), c, 1.0)
    sa_ref[...] = jnp.where(rot & (lane >= half) & (lane < 2 * half), s, 0.0)
    sb_ref[...] = jnp.where(rot & (lane < half), -s, 0.0)


def _rope_tables(length, head_dim, extra_rows):
    assert head_dim == LANES
    rot = head_dim // 4
    half = rot // 2
    inv = np.ones((1, LANES), np.float64)
    freqs = ROPE_THETA ** (-np.arange(0, rot, 2, dtype=np.float64) / rot)
    inv[0, :half] = freqs
    inv[0, half:rot] = freqs
    tr = min(length, 512)
    assert extra_rows % tr == 0
    out = jax.ShapeDtypeStruct((length + extra_rows, LANES), F32)
    return pl.pallas_call(
        functools.partial(_rope_tables_kernel, tr=tr, half=half, length=length),
        grid=((length + extra_rows) // tr,),
        in_specs=[pl.BlockSpec((1, LANES), lambda i: (0, 0))],
        out_specs=[pl.BlockSpec((tr, LANES), lambda i: (i, 0))] * 3,
        out_shape=[out, out, out],
        compiler_params=_params("parallel"),
        name="rope_tables",
    )(jnp.asarray(inv, dtype=F32))


def _qkv_kernel(h_ref, w_ref, ss_ref, cos_ref, sa_ref, sb_ref, o_ref, y_ref, *, half, dim):
    @pl.when(pl.program_id(0) == 0)
    def _():
        y_ref[...] = jnp.zeros_like(y_ref)

    r = lax.rsqrt(ss_ref[...] * (1.0 / dim) + NORM_EPS)
    cos, sa, sb = cos_ref[...], sa_ref[...], sb_ref[...]
    for c in range(y_ref.shape[1] // LANES):
        yc = y_ref[:, c * LANES:(c + 1) * LANES] * r
        out = yc * cos + pltpu.roll(yc, half, 1) * sa + pltpu.roll(yc, LANES - half, 1) * sb
        o_ref[:, c * LANES:(c + 1) * LANES] = out.astype(o_ref.dtype)
    y_ref[...] = jnp.dot(h_ref[...], w_ref[...], preferred_element_type=F32)


def _qkv_proj(h, ss, w, lay, head_dim, *, tn):
    m, k = h.shape
    n = w.shape[1]
    tm = lay.tm
    max_len = max(lay.lp, lay.ls)
    tables = _rope_tables(max_len, head_dim, tm)
    rope_tiles = (2 * n // 3) // tn
    steps, cur, prev = _lagged_tiles(m // tm, n // tn)

    def pos_block(s):
        i, j = prev(s)
        in_p = i < lay.tiles_p
        blk = jnp.where(in_p, i % (lay.lp // tm), (i - lay.tiles_p) % (lay.ls // tm))
        return jnp.where(j < rope_tiles, blk, max_len // tm), 0

    tab = pl.BlockSpec((tm, LANES), pos_block)
    return pl.pallas_call(
        functools.partial(_qkv_kernel, half=head_dim // 8, dim=k),
        grid=(steps + 1,),
        in_specs=[pl.BlockSpec((tm, k), lambda s: (cur(s)[0], 0)),
                  pl.BlockSpec((k, tn), lambda s: (0, cur(s)[1])),
                  pl.BlockSpec((tm, LANES), lambda s: (prev(s)[0], 0)),
                  tab, tab, tab],
        out_specs=pl.BlockSpec((tm, tn), prev),
        out_shape=jax.ShapeDtypeStruct((m, n), BF16),
        scratch_shapes=[pltpu.VMEM((tm, tn), F32)],
        compiler_params=_params("arbitrary"),
        name="qkv_proj",
    )(h, w, ss, *tables)


def _attn_kernel(lq1_ref, lk1_ref, lq2_ref, lk2_ref, q_ref, k_ref, v_ref, w_ref, *rest, hd, scale, lam_init, tk):
    o_ref = rest[-1]
    lam = (jnp.exp(jnp.sum(lq1_ref[...] * lk1_ref[...], keepdims=True))
           - jnp.exp(jnp.sum(lq2_ref[...] * lk2_ref[...], keepdims=True)) + lam_init)
    q = q_ref[...]
    qm = (q[:, :hd], q[:, hd:])
    dn = (((1,), (1,)), ((), ()))
    c2 = scale * LOG2_E
    m, l, acc = [None, None], [None, None], [None, None]
    for c in range(k_ref.shape[0] // tk):
        kc = k_ref[c * tk:(c + 1) * tk, :]
        vc = v_ref[c * tk:(c + 1) * tk, :]
        for mp in range(2):
            s = lax.dot_general(qm[mp], kc[:, mp * hd:(mp + 1) * hd], dn, preferred_element_type=F32) * c2
            mc = jnp.max(s, axis=-1, keepdims=True)
            m_new = mc if c == 0 else jnp.maximum(m[mp], mc)
            p = jnp.exp2(s - m_new)
            ls = jnp.sum(p, axis=-1, keepdims=True)
            pv = jnp.dot(p.astype(BF16), vc, preferred_element_type=F32)
            if c == 0:
                l[mp], acc[mp] = ls, pv
            else:
                alpha = jnp.exp2(m[mp] - m_new)
                l[mp] = alpha * l[mp] + ls
                acc[mp] = alpha * acc[mp] + pv
            m[mp] = m_new
    o = acc[0] * (1.0 / l[0]) - acc[1] * (lam / l[1])
    o = o * lax.rsqrt(jnp.mean(o * o, axis=-1, keepdims=True) + SUBLN_EPS)
    o_ref[...] = (o * w_ref[...] * (1.0 - lam_init)).astype(o_ref.dtype)


def _diff_attention(qkv, lams, subln_w, row0, n_seq, length, d_model, hd, layer, *, tq, dst=None):
    heads = d_model // (2 * hd)
    lam_init = 0.8 - 0.6 * math.exp(-LAMBDA_INIT_DECAY * layer)
    qt = length // tq
    qb0, sb0 = row0 // tq, row0 // length
    vec = pl.BlockSpec((1, hd), lambda b, h, i: (0, 0))
    kv_mode = pl.Buffered(1)
    extra_specs, extra_args, _ = _into(dst)
    return pl.pallas_call(
        functools.partial(_attn_kernel, hd=hd, scale=hd ** -0.5, lam_init=lam_init, tk=min(ATTN_KV_CHUNK, length)),
        grid=(n_seq, heads, qt),
        in_specs=[vec, vec, vec, vec,
                  pl.BlockSpec((tq, 2 * hd), lambda b, h, i: (qb0 + b * qt + i, h)),
                  pl.BlockSpec((length, 2 * hd), lambda b, h, i: (sb0 + b, heads + h), pipeline_mode=kv_mode),
                  pl.BlockSpec((length, 2 * hd), lambda b, h, i: (sb0 + b, 2 * heads + h), pipeline_mode=kv_mode),
                  pl.BlockSpec((1, 2 * hd), lambda b, h, i: (0, 0))] + extra_specs,
        out_specs=pl.BlockSpec((tq, 2 * hd), lambda b, h, i: (qb0 + b * qt + i, h)),
        out_shape=jax.ShapeDtypeStruct((qkv.shape[0], d_model), BF16),
        input_output_aliases={8: 0} if dst is not None else {},
        compiler_params=_params("parallel", "parallel", "arbitrary"),
        name="diff_attention",
    )(*[v.reshape(1, hd).astype(F32) for v in lams], qkv, qkv, qkv, subln_w.reshape(1, 2 * hd).astype(F32),
      *extra_args)


def _ffn_up_kernel(hp_ref, h_ref, hn_ref, sp_ref, s_ref, sn_ref, wg_ref, wu0_ref, wu1_ref, cw_ref, cb_ref, g_ref,
                   lhs_ref, r_ref, *, lay, halo, tn, f):
    i, j = pl.program_id(0), pl.program_id(1)
    tm = lay.tm

    @pl.when(j == 0)
    def _():
        length, pos0 = lay.seq_pos(i)
        zero = jnp.zeros(hp_ref.shape, hp_ref.dtype)
        lhs_ref[0:halo, :] = jnp.where(pos0 == 0, zero, hp_ref[...])
        lhs_ref[halo:halo + tm, :] = h_ref[...]
        lhs_ref[halo + tm:, :] = jnp.where(pos0 + tm == length, zero, hn_ref[...])
        inv_dim = 1.0 / h_ref.shape[1]
        r_ref[0:halo, :] = lax.rsqrt(sp_ref[...] * inv_dim + NORM_EPS)
        r_ref[halo:halo + tm, :] = lax.rsqrt(s_ref[...] * inv_dim + NORM_EPS)
        r_ref[halo + tm:, :] = lax.rsqrt(sn_ref[...] * inv_dim + NORM_EPS)

    n = tm + 2 * halo
    lhs = lhs_ref[...]
    cw, cb = cw_ref[...], cb_ref[...]
    r = jnp.concatenate([r_ref[...]] * (tn // LANES), axis=1)

    def conv(u, c0):
        k = slice(c0, c0 + tn)
        u = u * r
        return (pltpu.roll(u, 1, 0)[halo:halo + tm] * cw[0:1, k] + u[halo:halo + tm] * cw[1:2, k]
                + pltpu.roll(u, n - 1, 0)[halo:halo + tm] * cw[2:3, k] + cb[:, k])

    gate = conv(jnp.dot(lhs, wg_ref[...], preferred_element_type=F32), 0)
    up = conv(jnp.concatenate([jnp.dot(lhs, wu0_ref[...], preferred_element_type=F32),
                               jnp.dot(lhs, wu1_ref[...], preferred_element_type=F32)], axis=1), tn)
    g = gate / (1.0 + jnp.exp(-gate)) * up
    col = j * tn + lax.broadcasted_iota(jnp.int32, (1, tn), 1)
    g_ref[...] = jnp.where(col < f, g, 0.0).astype(g_ref.dtype)


def _gate_up_tiles(x, f, f_pad, tn):
    r = x.shape[0]
    x = jnp.pad(x.reshape(r, 2, f), ((0, 0), (0, 0), (0, f_pad - f)))
    return jnp.swapaxes(x.reshape(r, 2, f_pad // tn, tn), 1, 2).reshape(r, 2 * f_pad)


def _ffn_up(h, ss, w_up, layer, conv_w, conv_b, lay, *, tn, f_pad):
    m, k = h.shape
    f = w_up.shape[2] // 2
    half = tn // 2
    assert f % half == 0
    tm, halo = lay.tm, SUBLANES_BF16
    hb = tm // halo
    last_hb = m // halo - 1
    up0, last_half = f // half, 2 * f // half - 1

    def with_halo(width):
        return [pl.BlockSpec((halo, width), lambda i, j: (jnp.maximum(i * hb - 1, 0), 0)),
                pl.BlockSpec((tm, width), lambda i, j: (i, 0)),
                pl.BlockSpec((halo, width), lambda i, j: (jnp.minimum((i + 1) * hb, last_hb), 0))]

    return pl.pallas_call(
        functools.partial(_ffn_up_kernel, lay=lay, halo=halo, tn=tn, f=f),
        grid=(m // tm, f_pad // tn),
        in_specs=with_halo(k) + with_halo(LANES) + [
                  pl.BlockSpec((None, k, tn), lambda i, j: (layer, 0, j)),
                  pl.BlockSpec((None, k, half), lambda i, j: (layer, 0, up0 + 2 * j)),
                  pl.BlockSpec((None, k, half), lambda i, j: (layer, 0, jnp.minimum(up0 + 2 * j + 1, last_half))),
                  pl.BlockSpec((3, 2 * tn), lambda i, j: (0, j)),
                  pl.BlockSpec((1, 2 * tn), lambda i, j: (0, j))],
        out_specs=pl.BlockSpec((tm, tn), lambda i, j: (i, j)),
        out_shape=jax.ShapeDtypeStruct((m, f_pad), BF16),
        scratch_shapes=[pltpu.VMEM((tm + 2 * halo, k), BF16), pltpu.VMEM((tm + 2 * halo, LANES), F32)],
        compiler_params=_params("parallel", "arbitrary"),
        name="ffn_up",
    )(h, h, h, ss, ss, ss, w_up, w_up, w_up,
      _gate_up_tiles(conv_w.astype(F32), f, f_pad, tn), _gate_up_tiles(conv_b.astype(F32).reshape(1, 2 * f), f, f_pad, tn))


def _ffn_down_kernel(g_ref, w_ref, r_ref, o_ref, xb_ref, ss_ref):
    kk = pl.program_id(2)

    @pl.when(kk == 0)
    def _():
        o_ref[...] = r_ref[...]

    o_ref[...] += jnp.dot(g_ref[...], w_ref[...], preferred_element_type=F32)

    @pl.when(kk == pl.num_programs(2) - 1)
    def _():
        _emit_stream(o_ref[...], pl.program_id(1), o_ref, xb_ref, ss_ref)


def _ffn_down(g, w, layer, resid, *, tm, tn, tk):
    m, k = g.shape
    n = w.shape[2]
    return pl.pallas_call(
        _ffn_down_kernel,
        grid=(m // tm, n // tn, k // tk),
        in_specs=[pl.BlockSpec((tm, tk), lambda i, j, kk: (i, kk)),
                  pl.BlockSpec((None, tk, tn), lambda i, j, kk: (layer, kk, j)),
                  pl.BlockSpec((tm, tn), lambda i, j, kk: (i, j))],
        out_specs=[pl.BlockSpec((tm, tn), lambda i, j, kk: (i, j)),
                   pl.BlockSpec((tm, tn), lambda i, j, kk: (i, j)),
                   pl.BlockSpec((tm, LANES), lambda i, j, kk: (i, 0))],
        out_shape=[jax.ShapeDtypeStruct((m, n), F32), jax.ShapeDtypeStruct((m, n), BF16),
                   jax.ShapeDtypeStruct((m, LANES), F32)],
        compiler_params=_params("parallel", "arbitrary", "arbitrary"),
        name="ffn_down",
    )(g, w, resid)


def _down_k_tile(f_pad):
    units = f_pad // MXU_DIM
    best = 1
    for d in range(1, units + 1):
        if units % d == 0 and d * MXU_DIM * 4 <= max(f_pad, 4 * MXU_DIM):
            best = d
    return best * MXU_DIM


def kernel(x_prompt, x_sample, mix_norm_even, w_in_even, pool_w, pool_scale, fourier_w, w_out_even, mix_norm_odd,
           w_qkv, lambda_q1, lambda_k1, lambda_q2, lambda_k2, subln_w, w_out_odd, ffn_norm, w_up, conv_w, conv_b,
           w_down, final_norm):
    bp, lp, d = x_prompt.shape
    bs, ls, _ = x_sample.shape
    assert bp == 1, "the prompt group is handled as one sequence"
    lay = _Layout(bp, lp, bs, ls)
    tm = lay.tm
    depth = ffn_norm.shape[0]
    f = w_down.shape[1]
    tn_ffn = FFN_COL_TILE
    f_pad = -(-f // tn_ffn) * tn_ffn
    hd = lambda_q1.shape[1]
    pool_dim = pool_w.shape[1] * pool_w.shape[2]
    fourier_dim = fourier_w.shape[1]
    col_tile = min(512, d)

    w_up_b = (ffn_norm.astype(F32)[:, :, None] * w_up).astype(BF16)
    w_down_b = jnp.pad(w_down, ((0, 0), (0, f_pad - f), (0, 0))).astype(BF16)

    parts = [(x_prompt.reshape(lp, d), 0), (x_sample.reshape(bs * ls, d), lay.rows_p)]

    for layer in range(depth):
        i = layer // 2
        if layer % 2 == 0:
            h = _rmsnorm_parts(parts, mix_norm_even[i], lay.rows)
            u = _matmul(h, w_in_even[i].astype(BF16), out_dtype=F32, tm=tm, tn=col_tile, name="even_in_proj")
            a = _pool_mixer(u, pool_w[i].astype(BF16), pool_scale[i], lay)
            va, vb = _chan_dft(u, pool_dim, fourier_dim // FOURIER_HEAD_DIM, lay)
            fr = _seq_dft(va, vb, 0, bp, lp, MAX_ROW_TILE)
            fr = _seq_dft(va, vb, lay.rows_p, bs, ls, MAX_ROW_TILE, dst=fr)
            b = _matmul(fr, fourier_w[i].astype(BF16), out_dtype=BF16, tm=tm, tn=col_tile, name="fourier_proj")
            x, xb, ss = _out_proj2_parts(a, b, w_out_even[i].astype(BF16), parts, tm=tm, tn=col_tile)
        else:
            assert layer > 0, "an odd layer follows a conv-FFN, which supplies bf16(x) and the row sums"
            w_qkv_b = (mix_norm_odd[i].astype(F32)[:, None] * w_qkv[i]).astype(BF16)
            qkv = _qkv_proj(xb, ss, w_qkv_b, lay, hd, tn=min(QKV_COL_TILE, d))
            lams = (lambda_q1[i], lambda_k1[i], lambda_q2[i], lambda_k2[i])
            o = _diff_attention(qkv, lams, subln_w[i], 0, bp, lp, d, hd, layer, tq=min(ATTN_Q_TILE_LONG, lp))
            o = _diff_attention(qkv, lams, subln_w[i], lay.rows_p, bs, ls, d, hd, layer,
                                tq=min(ATTN_Q_TILE_SHORT, ls), dst=o)
            x, xb, ss = _matmul_resid_parts(o, w_out_odd[i].astype(BF16), parts, tm=tm, tn=col_tile,
                                            name="odd_out_proj")

        g = _ffn_up(xb, ss, w_up_b, layer, conv_w[layer], conv_b[layer], lay, tn=tn_ffn, f_pad=f_pad)
        x, xb, ss = _ffn_down(g, w_down_b, layer, x, tm=tm, tn=min(1024, d), tk=_down_k_tile(f_pad))
        parts = [(x, 0)]

    y_prompt = _rmsnorm(x, final_norm, out_dtype=F32, in_row0=0, rows=lay.rows_p)
    y_sample = _rmsnorm(x, final_norm, out_dtype=F32, in_row0=lay.rows_p, rows=lay.rows_s)
    return y_prompt.reshape(bp, lp, d), y_sample.reshape(bs, ls, d)
```

```python
import functools
import math

import numpy as np
import jax
import jax.numpy as jnp
from jax import lax
from jax.experimental import pallas as pl
from jax.experimental.pallas import tpu as pltpu

F32 = jnp.float32
BF16 = jnp.bfloat16

POOL_WINDOWS = (2, 4, 8, 16)
FOURIER_HEAD_DIM = 256
ROPE_THETA = 500000.0
NORM_EPS = 1e-6
SUBLN_EPS = 1e-5
LAMBDA_INIT_DECAY = 0.3

LANES = 128
SUBLANES_F32 = 8
SUBLANES_BF16 = 16
MXU_DIM = 256
VMEM_LIMIT_BYTES = 56 * 1024 * 1024

MAX_ROW_TILE = 1024
NORM_ROW_TILE = 256
FFN_COL_TILE = 512
QKV_COL_TILE = 1024
ATTN_KV_CHUNK = 512
ATTN_Q_TILE_LONG = 512
ATTN_Q_TILE_SHORT = 1024
DFT_ROW_SPLIT = 64
LOG2_E = math.log2(math.e)


def _params(*sem):
    return pltpu.CompilerParams(dimension_semantics=sem, vmem_limit_bytes=VMEM_LIMIT_BYTES)


class _Layout:
    def __init__(self, n_prompt, len_prompt, n_sample, len_sample):
        self.lp, self.ls = len_prompt, len_sample
        self.rows_p, self.rows_s = n_prompt * len_prompt, n_sample * len_sample
        self.rows = self.rows_p + self.rows_s
        self.tm = min(MAX_ROW_TILE, len_prompt, len_sample)
        assert len_prompt % self.tm == 0 and len_sample % self.tm == 0
        self.tiles_p = self.rows_p // self.tm
        self.tiles = self.rows // self.tm

    def seq_pos(self, i):
        in_p = i < self.tiles_p
        length = jnp.where(in_p, self.lp, self.ls)
        tile_in_seq = jnp.where(in_p, i % (self.lp // self.tm), (i - self.tiles_p) % (self.ls // self.tm))
        return length, tile_in_seq * self.tm


class _Rows:
    def __init__(self, *arrays):
        assert 1 <= len(arrays) <= 2
        self.arrays = arrays
        self.first_rows = arrays[0].shape[0]
        self.rows = sum(a.shape[0] for a in arrays)
        self.width = arrays[0].shape[1]

    def tiles_first(self, tm):
        assert self.first_rows % tm == 0
        return self.first_rows // tm

    def specs(self, tm, block_width, row=lambda *idx: idx[0], col=lambda *idx: 0):
        shape = (tm, block_width)
        if len(self.arrays) == 1:
            return [pl.BlockSpec(shape, lambda *idx: (row(*idx), col(*idx)))]
        tp = self.tiles_first(tm)
        return [pl.BlockSpec(shape, lambda *idx: (jnp.minimum(row(*idx), tp - 1), col(*idx))),
                pl.BlockSpec(shape, lambda *idx: (jnp.maximum(row(*idx) - tp, 0), col(*idx)))]


def _on_part(i, tiles_first, ref_groups, body):
    if len(ref_groups) == 1:
        body(*ref_groups[0])
        return
    pl.when(i < tiles_first)(lambda: body(*ref_groups[0]))
    pl.when(i >= tiles_first)(lambda: body(*ref_groups[1]))


def _lagged_tiles(n_i, n_j):
    n = n_i * n_j

    def cur(s):
        c = jnp.minimum(s, n - 1)
        return c // n_j, c % n_j

    def prev(s):
        p = jnp.maximum(s - 1, 0)
        return p // n_j, p % n_j

    return n, cur, prev


def _rmsnorm_kernel(*refs, eps, n_src, tiles_first):
    g_ref, o_ref = refs[n_src], refs[n_src + 1]

    def body(x_ref):
        x = x_ref[...]
        y = x * lax.rsqrt(jnp.mean(x * x, axis=-1, keepdims=True) + eps)
        o_ref[...] = (y * g_ref[...]).astype(o_ref.dtype)

    _on_part(pl.program_id(0), tiles_first, [(r,) for r in refs[:n_src]], body)


def _rmsnorm(src, g, *, out_dtype, row0=0, rows=None):
    d = src.width
    tm = NORM_ROW_TILE
    rows = src.rows - row0 if rows is None else rows
    rb0 = row0 // tm
    n_src = len(src.arrays)
    return pl.pallas_call(
        functools.partial(_rmsnorm_kernel, eps=NORM_EPS, n_src=n_src, tiles_first=src.tiles_first(tm)),
        grid=(rows // tm,),
        in_specs=src.specs(tm, d, row=lambda i: i + rb0) + [pl.BlockSpec((1, d), lambda i: (0, 0))],
        out_specs=pl.BlockSpec((tm, d), lambda i: (i, 0)),
        out_shape=jax.ShapeDtypeStruct((rows, d), out_dtype),
        compiler_params=_params("parallel"),
        name="rmsnorm",
    )(*src.arrays, g.reshape(1, d).astype(F32))


def _emit_stream(x, j, o_ref, xb_ref, ss_ref):
    if o_ref is not None:
        o_ref[...] = x
    xb_ref[...] = x.astype(xb_ref.dtype)
    part = jnp.broadcast_to(jnp.sum(x * x, axis=-1, keepdims=True), ss_ref.shape)

    @pl.when(j == 0)
    def _():
        ss_ref[...] = part

    @pl.when(j > 0)
    def _():
        ss_ref[...] += part


def _stream_outs(m, n, tm, tn):
    specs = [pl.BlockSpec((tm, tn), lambda i, j: (i, j)),
             pl.BlockSpec((tm, tn), lambda i, j: (i, j)),
             pl.BlockSpec((tm, LANES), lambda i, j: (i, 0))]
    shapes = [jax.ShapeDtypeStruct((m, n), F32), jax.ShapeDtypeStruct((m, n), BF16),
              jax.ShapeDtypeStruct((m, LANES), F32)]
    return specs, shapes


def _mm_kernel(*refs, n_a, n_r, tiles_first, stream):
    a_refs, b_ref = refs[:n_a], refs[n_a]
    r_refs = refs[n_a + 1:n_a + 1 + n_r]
    outs = refs[n_a + 1 + n_r:]
    i, j = pl.program_id(0), pl.program_id(1)

    def body(a_ref, r_ref):
        acc = jnp.dot(a_ref[...], b_ref[...], preferred_element_type=F32)
        if r_ref is not None:
            acc = acc + r_ref[...]
        if stream:
            _emit_stream(acc, j, *outs)
        else:
            outs[0][...] = acc.astype(outs[0].dtype)

    n_parts = max(n_a, n_r)
    groups = [(a_refs[min(p, n_a - 1)], r_refs[min(p, n_r - 1)] if n_r else None) for p in range(n_parts)]
    _on_part(i, tiles_first, groups, body)


def _matmul(a, b, *, tm, tn, name, out_dtype=None, resid=None):
    m, k = a.rows, a.width
    n = b.shape[1]
    srcs = [a] + ([resid] if resid is not None else [])
    firsts = {s.tiles_first(tm) for s in srcs if len(s.arrays) == 2}
    assert len(firsts) <= 1, "two-part operands must split at the same row"
    in_specs = a.specs(tm, k) + [pl.BlockSpec((k, tn), lambda i, j: (0, j))]
    args = list(a.arrays) + [b]
    if resid is not None:
        in_specs += resid.specs(tm, tn, col=lambda i, j: j)
        args += list(resid.arrays)
        out_specs, out_shape = _stream_outs(m, n, tm, tn)
    else:
        out_specs = pl.BlockSpec((tm, tn), lambda i, j: (i, j))
        out_shape = jax.ShapeDtypeStruct((m, n), out_dtype)
    return pl.pallas_call(
        functools.partial(_mm_kernel, n_a=len(a.arrays), n_r=len(resid.arrays) if resid is not None else 0,
                          tiles_first=firsts.pop() if firsts else 0, stream=resid is not None),
        grid=(m // tm, n // tn),
        in_specs=in_specs,
        out_specs=out_specs,
        out_shape=out_shape,
        compiler_params=_params("parallel", "arbitrary"),
        name=name,
    )(*args)


def _pool_kernel(up_ref, u_ref, un_ref, w_ref, s_ref, o_ref, *, lay, halo):
    g = pl.program_id(0)
    i = pl.program_id(1)
    tm = lay.tm
    length, pos0 = lay.seq_pos(i)
    x = u_ref[...]
    prev = jnp.where(pos0 == 0, 0.0, up_ref[...])
    nxt = jnp.where(pos0 + tm == length, 0.0, un_ref[...])
    e = jnp.concatenate([prev, x, nxt], axis=0)
    n = tm + 2 * halo
    s1 = e + pltpu.roll(e, 1, 0)
    t = pos0 + lax.broadcasted_iota(jnp.int32, (tm, 1), 0)

    for gi, w in enumerate(POOL_WINDOWS):
        h = w // 2

        @pl.when(g == gi)
        def _(h=h):
            s, k = s1, 1
            while k < h:
                s = pltpu.roll(s, k, 0) + pltpu.roll(s, n - k, 0)
                k *= 2
            win = s[halo:halo + tm]
            cnt = (jnp.minimum(t + h, length) - jnp.maximum(t - h, 0)).astype(F32)
            pooled = win / cnt - x
            y = jnp.dot(pooled.astype(BF16), w_ref[...], preferred_element_type=F32)
            o_ref[...] = (y * s_ref[...]).astype(o_ref.dtype)


def _pool_mixer(u, pool_w, pool_scale, lay):
    n_groups, c, _ = pool_w.shape
    tm, halo = lay.tm, SUBLANES_F32
    assert max(POOL_WINDOWS) // 2 <= halo and len(POOL_WINDOWS) == n_groups
    hb = tm // halo
    last_hb = lay.rows // halo - 1
    return pl.pallas_call(
        functools.partial(_pool_kernel, lay=lay, halo=halo),
        grid=(n_groups, lay.tiles),
        in_specs=[
            pl.BlockSpec((halo, c), lambda g, i: (jnp.maximum(i * hb - 1, 0), g)),
            pl.BlockSpec((tm, c), lambda g, i: (i, g)),
            pl.BlockSpec((halo, c), lambda g, i: (jnp.minimum((i + 1) * hb, last_hb), g)),
            pl.BlockSpec((None, c, c), lambda g, i: (g, 0, 0)),
            pl.BlockSpec((1, c), lambda g, i: (0, g)),
        ],
        out_specs=pl.BlockSpec((tm, c), lambda g, i: (i, g)),
        out_shape=jax.ShapeDtypeStruct((lay.rows, n_groups * c), BF16),
        compiler_params=_params("parallel", "parallel"),
        name="pool_mixer",
    )(u, u, u, pool_w, pool_scale.reshape(1, n_groups * c).astype(F32))


def _chan_dft_kernel(u_ref, w_ref, a_ref, b_ref, *, n):
    ab = jnp.dot(u_ref[...].astype(BF16), w_ref[...], preferred_element_type=F32)
    a_ref[...] = ab[:, :n].astype(a_ref.dtype)
    b_ref[...] = ab[:, n:].astype(b_ref.dtype)


def _chan_dft(u, col0, heads, lay):
    n = FOURIER_HEAD_DIM
    jk = np.outer(np.arange(n), np.arange(n)) % n
    ang = 2.0 * np.pi * jk / n
    w = jnp.asarray(np.concatenate([np.cos(ang), np.sin(ang)], axis=1) / math.sqrt(n), dtype=BF16)
    cb0 = col0 // n
    out = jax.ShapeDtypeStruct((lay.rows, heads * n), BF16)
    return pl.pallas_call(
        functools.partial(_chan_dft_kernel, n=n),
        grid=(heads, lay.tiles),
        in_specs=[pl.BlockSpec((lay.tm, n), lambda h, i: (i, cb0 + h)),
                  pl.BlockSpec((n, 2 * n), lambda h, i: (0, 0))],
        out_specs=[pl.BlockSpec((lay.tm, n), lambda h, i: (i, h))] * 2,
        out_shape=[out, out],
        compiler_params=_params("parallel", "parallel"),
        name="chan_dft",
    )(u, w)


def _trig_rows_kernel(c_ref, s_ref, *, length, mult, tr, tc):
    r = (lax.broadcasted_iota(jnp.int32, (tr, tc), 0) + pl.program_id(0) * tr) * mult
    c = lax.broadcasted_iota(jnp.int32, (tr, tc), 1) + pl.program_id(1) * tc
    ang = ((r * c) & (length - 1)).astype(F32) * (2.0 * math.pi / length)
    c_ref[...] = jnp.cos(ang)
    s_ref[...] = jnp.sin(ang)


def _trig_rows(length, n_rows, mult):
    tr, tc = SUBLANES_F32, min(length, 2048)
    out = jax.ShapeDtypeStruct((n_rows, length), F32)
    return pl.pallas_call(
        functools.partial(_trig_rows_kernel, length=length, mult=mult, tr=tr, tc=tc),
        grid=(n_rows // tr, length // tc),
        out_specs=[pl.BlockSpec((tr, tc), lambda i, j: (i, j))] * 2,
        out_shape=[out, out],
        compiler_params=_params("parallel", "parallel"),
        name="trig_rows",
    )()


def _dft_mat_kernel(ca_ref, sa_ref, cb_ref, sb_ref, c_ref, s_ref, *, split, groups):
    cb, sb = cb_ref[...], sb_ref[...]
    for q in range(groups):
        ca, sa = ca_ref[q:q + 1, :], sa_ref[q:q + 1, :]
        rows = slice(q * split, (q + 1) * split)
        c_ref[rows, :] = (ca * cb - sa * sb).astype(c_ref.dtype)
        s_ref[rows, :] = (-(sa * cb + ca * sb)).astype(s_ref.dtype)


def _dft_matrices(length):
    assert length & (length - 1) == 0
    split = min(DFT_ROW_SPLIT, length // SUBLANES_F32)
    groups = SUBLANES_F32
    ca, sa = _trig_rows(length, length // split, split)
    cb, sb = _trig_rows(length, split, 1)
    tc = min(length, 2048)
    out = jax.ShapeDtypeStruct((length, length), BF16)
    return pl.pallas_call(
        functools.partial(_dft_mat_kernel, split=split, groups=groups),
        grid=(length // (split * groups), length // tc),
        in_specs=[pl.BlockSpec((groups, tc), lambda i, j: (i, j))] * 2
        + [pl.BlockSpec((split, tc), lambda i, j: (0, j))] * 2,
        out_specs=[pl.BlockSpec((split * groups, tc), lambda i, j: (i, j))] * 2,
        out_shape=[out, out],
        compiler_params=_params("parallel", "parallel"),
        name="dft_matrices",
    )(ca, sa, cb, sb)


def _seq_dft_kernel(c_ref, s_ref, a_ref, b_ref, o_ref, acc_ref, *, scale):
    k = pl.program_id(2)

    @pl.when(k == 0)
    def _():
        acc_ref[...] = jnp.zeros_like(acc_ref)

    acc_ref[...] += (jnp.dot(c_ref[...], a_ref[...], preferred_element_type=F32)
                     + jnp.dot(s_ref[...], b_ref[...], preferred_element_type=F32))

    @pl.when(k == pl.num_programs(2) - 1)
    def _():
        o_ref[...] = (acc_ref[...] * scale).astype(o_ref.dtype)


def _seq_dft(va, vb, row0, n_seq, length, tile):
    cmat, nsmat = _dft_matrices(length)
    n = va.shape[1]
    tm = tk = min(tile, length)
    nt = length // tm
    rb0 = row0 // tk
    return pl.pallas_call(
        functools.partial(_seq_dft_kernel, scale=1.0 / math.sqrt(length)),
        grid=(n_seq, nt, nt),
        in_specs=[pl.BlockSpec((tm, tk), lambda s, i, k: (i, k)),
                  pl.BlockSpec((tm, tk), lambda s, i, k: (i, k)),
                  pl.BlockSpec((tk, n), lambda s, i, k: (rb0 + s * nt + k, 0)),
                  pl.BlockSpec((tk, n), lambda s, i, k: (rb0 + s * nt + k, 0))],
        out_specs=pl.BlockSpec((tm, n), lambda s, i, k: (s * nt + i, 0)),
        out_shape=jax.ShapeDtypeStruct((n_seq * length, n), BF16),
        scratch_shapes=[pltpu.VMEM((tm, n), F32)],
        compiler_params=_params("parallel", "parallel", "arbitrary"),
        name="seq_dft",
    )(cmat, nsmat, va, vb)


def _out_proj2_kernel(a_ref, b_ref, w1_ref, w2_ref, *rest, n_r, tiles_first):
    r_refs, outs = rest[:n_r], rest[n_r:]

    def body(r_ref):
        acc = jnp.dot(a_ref[...], w1_ref[...], preferred_element_type=F32)
        acc = acc + jnp.dot(b_ref[...], w2_ref[...], preferred_element_type=F32)
        _emit_stream(acc + r_ref[...], pl.program_id(1), *outs)

    _on_part(pl.program_id(0), tiles_first, [(r,) for r in r_refs], body)


def _out_proj2(a, b, w, resid, *, tm, tn):
    m, ka = a.shape
    kb = b.shape[1]
    n = w.shape[1]
    assert ka % kb == 0
    out_specs, out_shape = _stream_outs(m, n, tm, tn)
    return pl.pallas_call(
        functools.partial(_out_proj2_kernel, n_r=len(resid.arrays), tiles_first=resid.tiles_first(tm)),
        grid=(m // tm, n // tn),
        in_specs=[pl.BlockSpec((tm, ka), lambda i, j: (i, 0)),
                  pl.BlockSpec((tm, kb), lambda i, j: (i, 0)),
                  pl.BlockSpec((ka, tn), lambda i, j: (0, j)),
                  pl.BlockSpec((kb, tn), lambda i, j: (ka // kb, j))] + resid.specs(tm, tn, col=lambda i, j: j),
        out_specs=out_specs,
        out_shape=out_shape,
        compiler_params=_params("parallel", "arbitrary"),
        name="even_out_proj",
    )(a, b, w, w, *resid.arrays)


def _rope_tables_kernel(inv_ref, cos_ref, sa_ref, sb_ref, *, tr, half, length):
    row = lax.broadcasted_iota(jnp.int32, (tr, LANES), 0) + pl.program_id(0) * tr
    lane = lax.broadcasted_iota(jnp.int32, (tr, LANES), 1)
    ang = row.astype(F32) * inv_ref[...]
    c, s = jnp.cos(ang), jnp.sin(ang)
    rot = row < length
    cos_ref[...] = jnp.where(rot & (lane < 2 * half), c, 1.0)
    sa_ref[...] = jnp.where(rot & (lane >= half) & (lane < 2 * half), s, 0.0)
    sb_ref[...] = jnp.where(rot & (lane < half), -s, 0.0)


def _rope_tables(length, head_dim, extra_rows):
    assert head_dim == LANES
    rot = head_dim // 4
    half = rot // 2
    inv = np.ones((1, LANES), np.float64)
    freqs = ROPE_THETA ** (-np.arange(0, rot, 2, dtype=np.float64) / rot)
    inv[0, :half] = freqs
    inv[0, half:rot] = freqs
    tr = min(length, 512)
    assert extra_rows % tr == 0
    out = jax.ShapeDtypeStruct((length + extra_rows, LANES), F32)
    return pl.pallas_call(
        functools.partial(_rope_tables_kernel, tr=tr, half=half, length=length),
        grid=((length + extra_rows) // tr,),
        in_specs=[pl.BlockSpec((1, LANES), lambda i: (0, 0))],
        out_specs=[pl.BlockSpec((tr, LANES), lambda i: (i, 0))] * 3,
        out_shape=[out, out, out],
        compiler_params=_params("parallel"),
        name="rope_tables",
    )(jnp.asarray(inv, dtype=F32))


def _qkv_kernel(h_ref, w_ref, ss_ref, cos_ref, sa_ref, sb_ref, o_ref, y_ref, *, half, dim):
    @pl.when(pl.program_id(0) == 0)
    def _():
        y_ref[...] = jnp.zeros_like(y_ref)

    r = lax.rsqrt(ss_ref[...] * (1.0 / dim) + NORM_EPS)
    cos, sa, sb = cos_ref[...], sa_ref[...], sb_ref[...]
    for c in range(y_ref.shape[1] // LANES):
        yc = y_ref[:, c * LANES:(c + 1) * LANES] * r
        out = yc * cos + pltpu.roll(yc, half, 1) * sa + pltpu.roll(yc, LANES - half, 1) * sb
        o_ref[:, c * LANES:(c + 1) * LANES] = out.astype(o_ref.dtype)
    y_ref[...] = jnp.dot(h_ref[...], w_ref[...], preferred_element_type=F32)


def _qkv_proj(h, ss, w, lay, head_dim, *, tn):
    m, k = h.shape
    n = w.shape[1]
    tm = lay.tm
    max_len = max(lay.lp, lay.ls)
    tables = _rope_tables(max_len, head_dim, tm)
    rope_tiles = (2 * n // 3) // tn
    steps, cur, prev = _lagged_tiles(m // tm, n // tn)

    def pos_block(s):
        i, j = prev(s)
        in_p = i < lay.tiles_p
        blk = jnp.where(in_p, i % (lay.lp // tm), (i - lay.tiles_p) % (lay.ls // tm))
        return jnp.where(j < rope_tiles, blk, max_len // tm), 0

    tab = pl.BlockSpec((tm, LANES), pos_block)
    return pl.pallas_call(
        functools.partial(_qkv_kernel, half=head_dim // 8, dim=k),
        grid=(steps + 1,),
        in_specs=[pl.BlockSpec((tm, k), lambda s: (cur(s)[0], 0)),
                  pl.BlockSpec((k, tn), lambda s: (0, cur(s)[1])),
                  pl.BlockSpec((tm, LANES), lambda s: (prev(s)[0], 0)),
                  tab, tab, tab],
        out_specs=pl.BlockSpec((tm, tn), prev),
        out_shape=jax.ShapeDtypeStruct((m, n), BF16),
        scratch_shapes=[pltpu.VMEM((tm, tn), F32)],
        compiler_params=_params("arbitrary"),
        name="qkv_proj",
    )(h, w, ss, *tables)


def _attn_kernel(lq1_ref, lk1_ref, lq2_ref, lk2_ref, q_ref, k_ref, v_ref, w_ref, o_ref, *, hd, scale, lam_init, tk):
    lam = (jnp.exp(jnp.sum(lq1_ref[...] * lk1_ref[...], keepdims=True))
           - jnp.exp(jnp.sum(lq2_ref[...] * lk2_ref[...], keepdims=True)) + lam_init)
    q = q_ref[...]
    qm = (q[:, :hd], q[:, hd:])
    dn = (((1,), (1,)), ((), ()))
    c2 = scale * LOG2_E
    m, l, acc = [None, None], [None, None], [None, None]
    for c in range(k_ref.shape[0] // tk):
        kc = k_ref[c * tk:(c + 1) * tk, :]
        vc = v_ref[c * tk:(c + 1) * tk, :]
        for mp in range(2):
            s = lax.dot_general(qm[mp], kc[:, mp * hd:(mp + 1) * hd], dn, preferred_element_type=F32) * c2
            mc = jnp.max(s, axis=-1, keepdims=True)
            m_new = mc if c == 0 else jnp.maximum(m[mp], mc)
            p = jnp.exp2(s - m_new)
            ls = jnp.sum(p, axis=-1, keepdims=True)
            pv = jnp.dot(p.astype(BF16), vc, preferred_element_type=F32)
            if c == 0:
                l[mp], acc[mp] = ls, pv
            else:
                alpha = jnp.exp2(m[mp] - m_new)
                l[mp] = alpha * l[mp] + ls
                acc[mp] = alpha * acc[mp] + pv
            m[mp] = m_new
    o = acc[0] * (1.0 / l[0]) - acc[1] * (lam / l[1])
    o = o * lax.rsqrt(jnp.mean(o * o, axis=-1, keepdims=True) + SUBLN_EPS)
    o_ref[...] = (o * w_ref[...] * (1.0 - lam_init)).astype(o_ref.dtype)


def _diff_attention(qkv, lams, subln_w, row0, n_seq, length, d_model, hd, layer, *, tq):
    heads = d_model // (2 * hd)
    lam_init = 0.8 - 0.6 * math.exp(-LAMBDA_INIT_DECAY * layer)
    qt = length // tq
    qb0, sb0 = row0 // tq, row0 // length
    vec = pl.BlockSpec((1, hd), lambda b, h, i: (0, 0))
    kv_mode = pl.Buffered(1)
    return pl.pallas_call(
        functools.partial(_attn_kernel, hd=hd, scale=hd ** -0.5, lam_init=lam_init, tk=min(ATTN_KV_CHUNK, length)),
        grid=(n_seq, heads, qt),
        in_specs=[vec, vec, vec, vec,
                  pl.BlockSpec((tq, 2 * hd), lambda b, h, i: (qb0 + b * qt + i, h)),
                  pl.BlockSpec((length, 2 * hd), lambda b, h, i: (sb0 + b, heads + h), pipeline_mode=kv_mode),
                  pl.BlockSpec((length, 2 * hd), lambda b, h, i: (sb0 + b, 2 * heads + h), pipeline_mode=kv_mode),
                  pl.BlockSpec((1, 2 * hd), lambda b, h, i: (0, 0))],
        out_specs=pl.BlockSpec((tq, 2 * hd), lambda b, h, i: (b * qt + i, h)),
        out_shape=jax.ShapeDtypeStruct((n_seq * length, d_model), BF16),
        compiler_params=_params("parallel", "parallel", "arbitrary"),
        name="diff_attention",
    )(*[v.reshape(1, hd).astype(F32) for v in lams], qkv, qkv, qkv, subln_w.reshape(1, 2 * hd).astype(F32))


def _ffn_up_kernel(hp_ref, h_ref, hn_ref, sp_ref, s_ref, sn_ref, wg_ref, wu0_ref, wu1_ref, cw_ref, cb_ref, g_ref,
                   lhs_ref, r_ref, *, lay, halo, tn, f):
    i, j = pl.program_id(0), pl.program_id(1)
    tm = lay.tm

    @pl.when(j == 0)
    def _():
        length, pos0 = lay.seq_pos(i)
        zero = jnp.zeros(hp_ref.shape, hp_ref.dtype)
        lhs_ref[0:halo, :] = jnp.where(pos0 == 0, zero, hp_ref[...])
        lhs_ref[halo:halo + tm, :] = h_ref[...]
        lhs_ref[halo + tm:, :] = jnp.where(pos0 + tm == length, zero, hn_ref[...])
        inv_dim = 1.0 / h_ref.shape[1]
        r_ref[0:halo, :] = lax.rsqrt(sp_ref[...] * inv_dim + NORM_EPS)
        r_ref[halo:halo + tm, :] = lax.rsqrt(s_ref[...] * inv_dim + NORM_EPS)
        r_ref[halo + tm:, :] = lax.rsqrt(sn_ref[...] * inv_dim + NORM_EPS)

    n = tm + 2 * halo
    lhs = lhs_ref[...]
    cw, cb = cw_ref[...], cb_ref[...]
    r = jnp.concatenate([r_ref[...]] * (tn // LANES), axis=1)

    def conv(u, c0):
        k = slice(c0, c0 + tn)
        u = u * r
        return (pltpu.roll(u, 1, 0)[halo:halo + tm] * cw[0:1, k] + u[halo:halo + tm] * cw[1:2, k]
                + pltpu.roll(u, n - 1, 0)[halo:halo + tm] * cw[2:3, k] + cb[:, k])

    gate = conv(jnp.dot(lhs, wg_ref[...], preferred_element_type=F32), 0)
    up = conv(jnp.concatenate([jnp.dot(lhs, wu0_ref[...], preferred_element_type=F32),
                               jnp.dot(lhs, wu1_ref[...], preferred_element_type=F32)], axis=1), tn)
    g = gate / (1.0 + jnp.exp(-gate)) * up
    col = j * tn + lax.broadcasted_iota(jnp.int32, (1, tn), 1)
    g_ref[...] = jnp.where(col < f, g, 0.0).astype(g_ref.dtype)


def _gate_up_tiles(x, f, f_pad, tn):
    r = x.shape[0]
    x = jnp.pad(x.reshape(r, 2, f), ((0, 0), (0, 0), (0, f_pad - f)))
    return jnp.swapaxes(x.reshape(r, 2, f_pad // tn, tn), 1, 2).reshape(r, 2 * f_pad)


def _ffn_up(h, ss, w_up, layer, conv_w, conv_b, lay, *, tn, f_pad):
    m, k = h.shape
    f = w_up.shape[2] // 2
    half = tn // 2
    assert f % half == 0
    tm, halo = lay.tm, SUBLANES_BF16
    hb = tm // halo
    last_hb = m // halo - 1
    up0, last_half = f // half, 2 * f // half - 1

    def with_halo(width):
        return [pl.BlockSpec((halo, width), lambda i, j: (jnp.maximum(i * hb - 1, 0), 0)),
                pl.BlockSpec((tm, width), lambda i, j: (i, 0)),
                pl.BlockSpec((halo, width), lambda i, j: (jnp.minimum((i + 1) * hb, last_hb), 0))]

    return pl.pallas_call(
        functools.partial(_ffn_up_kernel, lay=lay, halo=halo, tn=tn, f=f),
        grid=(m // tm, f_pad // tn),
        in_specs=with_halo(k) + with_halo(LANES) + [
                  pl.BlockSpec((None, k, tn), lambda i, j: (layer, 0, j)),
                  pl.BlockSpec((None, k, half), lambda i, j: (layer, 0, up0 + 2 * j)),
                  pl.BlockSpec((None, k, half), lambda i, j: (layer, 0, jnp.minimum(up0 + 2 * j + 1, last_half))),
                  pl.BlockSpec((3, 2 * tn), lambda i, j: (0, j)),
                  pl.BlockSpec((1, 2 * tn), lambda i, j: (0, j))],
        out_specs=pl.BlockSpec((tm, tn), lambda i, j: (i, j)),
        out_shape=jax.ShapeDtypeStruct((m, f_pad), BF16),
        scratch_shapes=[pltpu.VMEM((tm + 2 * halo, k), BF16), pltpu.VMEM((tm + 2 * halo, LANES), F32)],
        compiler_params=_params("parallel", "arbitrary"),
        name="ffn_up",
    )(h, h, h, ss, ss, ss, w_up, w_up, w_up,
      _gate_up_tiles(conv_w.astype(F32), f, f_pad, tn), _gate_up_tiles(conv_b.astype(F32).reshape(1, 2 * f), f, f_pad, tn))


def _ffn_down_kernel(g_ref, w_ref, r_ref, o_ref, xb_ref, ss_ref):
    kk = pl.program_id(2)

    @pl.when(kk == 0)
    def _():
        o_ref[...] = r_ref[...]

    o_ref[...] += jnp.dot(g_ref[...], w_ref[...], preferred_element_type=F32)

    @pl.when(kk == pl.num_programs(2) - 1)
    def _():
        _emit_stream(o_ref[...], pl.program_id(1), None, xb_ref, ss_ref)


def _ffn_down(g, w, layer, resid, *, tm, tn, tk):
    m, k = g.shape
    n = w.shape[2]
    return pl.pallas_call(
        _ffn_down_kernel,
        grid=(m // tm, n // tn, k // tk),
        in_specs=[pl.BlockSpec((tm, tk), lambda i, j, kk: (i, kk)),
                  pl.BlockSpec((None, tk, tn), lambda i, j, kk: (layer, kk, j)),
                  pl.BlockSpec((tm, tn), lambda i, j, kk: (i, j))],
        out_specs=[pl.BlockSpec((tm, tn), lambda i, j, kk: (i, j)),
                   pl.BlockSpec((tm, tn), lambda i, j, kk: (i, j)),
                   pl.BlockSpec((tm, LANES), lambda i, j, kk: (i, 0))],
        out_shape=[jax.ShapeDtypeStruct((m, n), F32), jax.ShapeDtypeStruct((m, n), BF16),
                   jax.ShapeDtypeStruct((m, LANES), F32)],
        compiler_params=_params("parallel", "arbitrary", "arbitrary"),
        name="ffn_down",
    )(g, w, resid)


def _down_k_tile(f_pad):
    units = f_pad // MXU_DIM
    best = 1
    for d in range(1, units + 1):
        if units % d == 0 and d * MXU_DIM * 4 <= max(f_pad, 4 * MXU_DIM):
            best = d
    return best * MXU_DIM


def kernel(x_prompt, x_sample, mix_norm_even, w_in_even, pool_w, pool_scale, fourier_w, w_out_even, mix_norm_odd,
           w_qkv, lambda_q1, lambda_k1, lambda_q2, lambda_k2, subln_w, w_out_odd, ffn_norm, w_up, conv_w, conv_b,
           w_down, final_norm):
    bp, lp, d = x_prompt.shape
    bs, ls, _ = x_sample.shape
    assert bp == 1, "the prompt group is handled as one sequence"
    lay = _Layout(bp, lp, bs, ls)
    tm = lay.tm
    depth = ffn_norm.shape[0]
    f = w_down.shape[1]
    tn_ffn = FFN_COL_TILE
    f_pad = -(-f // tn_ffn) * tn_ffn
    hd = lambda_q1.shape[1]
    pool_dim = pool_w.shape[1] * pool_w.shape[2]
    fourier_dim = fourier_w.shape[1]
    col_tile = min(512, d)

    w_up_b = (ffn_norm.astype(F32)[:, :, None] * w_up).astype(BF16)
    w_down_b = jnp.pad(w_down, ((0, 0), (0, f_pad - f), (0, 0))).astype(BF16)

    x = _Rows(x_prompt.reshape(lp, d), x_sample.reshape(bs * ls, d))

    for layer in range(depth):
        i = layer // 2
        if layer % 2 == 0:
            h = _rmsnorm(x, mix_norm_even[i], out_dtype=BF16)
            u = _matmul(_Rows(h), w_in_even[i].astype(BF16), out_dtype=F32, tm=tm, tn=col_tile, name="even_in_proj")
            a = _pool_mixer(u, pool_w[i].astype(BF16), pool_scale[i], lay)
            va, vb = _chan_dft(u, pool_dim, fourier_dim // FOURIER_HEAD_DIM, lay)
            fr = _Rows(_seq_dft(va, vb, 0, bp, lp, MAX_ROW_TILE), _seq_dft(va, vb, lay.rows_p, bs, ls, MAX_ROW_TILE))
            b = _matmul(fr, fourier_w[i].astype(BF16), out_dtype=BF16, tm=tm, tn=col_tile, name="fourier_proj")
            x_new, xb, ss = _out_proj2(a, b, w_out_even[i].astype(BF16), x, tm=tm, tn=col_tile)
        else:
            assert layer > 0, "an odd layer follows a conv-FFN, which supplies bf16(x) and the row sums"
            w_qkv_b = (mix_norm_odd[i].astype(F32)[:, None] * w_qkv[i]).astype(BF16)
            qkv = _qkv_proj(xb, ss, w_qkv_b, lay, hd, tn=min(QKV_COL_TILE, d))
            lams = (lambda_q1[i], lambda_k1[i], lambda_q2[i], lambda_k2[i])
            o = _Rows(_diff_attention(qkv, lams, subln_w[i], 0, bp, lp, d, hd, layer, tq=min(ATTN_Q_TILE_LONG, lp)),
                      _diff_attention(qkv, lams, subln_w[i], lay.rows_p, bs, ls, d, hd, layer,
                                      tq=min(ATTN_Q_TILE_SHORT, ls)))
            x_new, xb, ss = _matmul(o, w_out_odd[i].astype(BF16), resid=x, tm=tm, tn=col_tile, name="odd_out_proj")

        g = _ffn_up(xb, ss, w_up_b, layer, conv_w[layer], conv_b[layer], lay, tn=tn_ffn, f_pad=f_pad)
        x_new, xb, ss = _ffn_down(g, w_down_b, layer, x_new, tm=tm, tn=min(1024, d), tk=_down_k_tile(f_pad))
        x = _Rows(x_new)

    y_prompt = _rmsnorm(x, final_norm, out_dtype=F32, row0=0, rows=lay.rows_p)
    y_sample = _rmsnorm(x, final_norm, out_dtype=F32, row0=lay.rows_p, rows=lay.rows_s)
    return y_prompt.reshape(bp, lp, d), y_sample.reshape(bs, ls, d)
```

```python
import functools
import math

import numpy as np
import jax
import jax.numpy as jnp
from jax import lax
from jax.experimental import pallas as pl
from jax.experimental.pallas import tpu as pltpu

F32 = jnp.float32
BF16 = jnp.bfloat16

POOL_WINDOWS = (2, 4, 8, 16)
FOURIER_HEAD_DIM = 256
ROPE_THETA = 500000.0
NORM_EPS = 1e-6
SUBLN_EPS = 1e-5
LAMBDA_INIT_DECAY = 0.3

LANES = 128
SUBLANES_F32 = 8
SUBLANES_BF16 = 16
MXU_DIM = 256
VMEM_LIMIT_BYTES = 56 * 1024 * 1024

MAX_ROW_TILE = 1024
NORM_ROW_TILE = 256
FFN_COL_TILE = 512
QKV_COL_TILE = 1024
ATTN_KV_CHUNK = 512
ATTN_Q_TILE_LONG = 512
ATTN_Q_TILE_SHORT = 1024
DFT_ROW_SPLIT = 64
LOG2_E = math.log2(math.e)


def _params(*sem):
    return pltpu.CompilerParams(dimension_semantics=sem, vmem_limit_bytes=VMEM_LIMIT_BYTES)


class _Layout:
    def __init__(self, n_prompt, len_prompt, n_sample, len_sample):
        self.lp, self.ls = len_prompt, len_sample
        self.rows_p, self.rows_s = n_prompt * len_prompt, n_sample * len_sample
        self.rows = self.rows_p + self.rows_s
        self.tm = min(MAX_ROW_TILE, len_prompt, len_sample)
        assert len_prompt % self.tm == 0 and len_sample % self.tm == 0
        self.tiles_p = self.rows_p // self.tm
        self.tiles = self.rows // self.tm

    def seq_pos(self, i):
        in_p = i < self.tiles_p
        length = jnp.where(in_p, self.lp, self.ls)
        tile_in_seq = jnp.where(in_p, i % (self.lp // self.tm), (i - self.tiles_p) % (self.ls // self.tm))
        return length, tile_in_seq * self.tm


class _Rows:
    def __init__(self, *arrays):
        assert 1 <= len(arrays) <= 2
        self.arrays = arrays
        self.first_rows = arrays[0].shape[0]
        self.rows = sum(a.shape[0] for a in arrays)
        self.width = arrays[0].shape[1]

    def tiles_first(self, tm):
        assert self.first_rows % tm == 0
        return self.first_rows // tm

    def specs(self, tm, block_width, row=lambda *idx: idx[0], col=lambda *idx: 0):
        shape = (tm, block_width)
        if len(self.arrays) == 1:
            return [pl.BlockSpec(shape, lambda *idx: (row(*idx), col(*idx)))]
        tp = self.tiles_first(tm)
        return [pl.BlockSpec(shape, lambda *idx: (jnp.minimum(row(*idx), tp - 1), col(*idx))),
                pl.BlockSpec(shape, lambda *idx: (jnp.maximum(row(*idx) - tp, 0), col(*idx)))]


def _on_part(i, tiles_first, ref_groups, body):
    if len(ref_groups) == 1:
        body(*ref_groups[0])
        return
    pl.when(i < tiles_first)(lambda: body(*ref_groups[0]))
    pl.when(i >= tiles_first)(lambda: body(*ref_groups[1]))


def _lagged_tiles(n_i, n_j):
    n = n_i * n_j

    def cur(s):
        c = jnp.minimum(s, n - 1)
        return c // n_j, c % n_j

    def prev(s):
        p = jnp.maximum(s - 1, 0)
        return p // n_j, p % n_j

    return n, cur, prev


def _rmsnorm_kernel(*refs, eps, n_src, tiles_first):
    g_ref, o_ref = refs[n_src], refs[n_src + 1]

    def body(x_ref):
        x = x_ref[...]
        y = x * lax.rsqrt(jnp.mean(x * x, axis=-1, keepdims=True) + eps)
        o_ref[...] = (y * g_ref[...]).astype(o_ref.dtype)

    _on_part(pl.program_id(0), tiles_first, [(r,) for r in refs[:n_src]], body)


def _rmsnorm(src, g, *, out_dtype, row0=0, rows=None):
    d = src.width
    tm = NORM_ROW_TILE
    rows = src.rows - row0 if rows is None else rows
    rb0 = row0 // tm
    n_src = len(src.arrays)
    return pl.pallas_call(
        functools.partial(_rmsnorm_kernel, eps=NORM_EPS, n_src=n_src, tiles_first=src.tiles_first(tm)),
        grid=(rows // tm,),
        in_specs=src.specs(tm, d, row=lambda i: i + rb0) + [pl.BlockSpec((1, d), lambda i: (0, 0))],
        out_specs=pl.BlockSpec((tm, d), lambda i: (i, 0)),
        out_shape=jax.ShapeDtypeStruct((rows, d), out_dtype),
        compiler_params=_params("parallel"),
        name="rmsnorm",
    )(*src.arrays, g.reshape(1, d).astype(F32))


def _emit_stream(x, j, o_ref, xb_ref, ss_ref):
    if o_ref is not None:
        o_ref[...] = x
    xb_ref[...] = x.astype(xb_ref.dtype)
    part = jnp.broadcast_to(jnp.sum(x * x, axis=-1, keepdims=True), ss_ref.shape)

    @pl.when(j == 0)
    def _():
        ss_ref[...] = part

    @pl.when(j > 0)
    def _():
        ss_ref[...] += part


def _stream_outs(m, n, tm, tn):
    specs = [pl.BlockSpec((tm, tn), lambda i, j: (i, j)),
             pl.BlockSpec((tm, tn), lambda i, j: (i, j)),
             pl.BlockSpec((tm, LANES), lambda i, j: (i, 0))]
    shapes = [jax.ShapeDtypeStruct((m, n), F32), jax.ShapeDtypeStruct((m, n), BF16),
              jax.ShapeDtypeStruct((m, LANES), F32)]
    return specs, shapes


def _mm_kernel(*refs, n_a, n_r, tiles_first, stream):
    a_refs, b_ref = refs[:n_a], refs[n_a]
    r_refs = refs[n_a + 1:n_a + 1 + n_r]
    outs = refs[n_a + 1 + n_r:]
    i, j = pl.program_id(0), pl.program_id(1)

    def body(a_ref, r_ref):
        acc = jnp.dot(a_ref[...], b_ref[...], preferred_element_type=F32)
        if r_ref is not None:
            acc = acc + r_ref[...]
        if stream:
            _emit_stream(acc, j, *outs)
        else:
            outs[0][...] = acc.astype(outs[0].dtype)

    n_parts = max(n_a, n_r)
    groups = [(a_refs[min(p, n_a - 1)], r_refs[min(p, n_r - 1)] if n_r else None) for p in range(n_parts)]
    _on_part(i, tiles_first, groups, body)


def _matmul(a, b, *, tm, tn, name, out_dtype=None, resid=None):
    m, k = a.rows, a.width
    n = b.shape[1]
    srcs = [a] + ([resid] if resid is not None else [])
    firsts = {s.tiles_first(tm) for s in srcs if len(s.arrays) == 2}
    assert len(firsts) <= 1, "two-part operands must split at the same row"
    in_specs = a.specs(tm, k) + [pl.BlockSpec((k, tn), lambda i, j: (0, j))]
    args = list(a.arrays) + [b]
    if resid is not None:
        in_specs += resid.specs(tm, tn, col=lambda i, j: j)
        args += list(resid.arrays)
        out_specs, out_shape = _stream_outs(m, n, tm, tn)
    else:
        out_specs = pl.BlockSpec((tm, tn), lambda i, j: (i, j))
        out_shape = jax.ShapeDtypeStruct((m, n), out_dtype)
    return pl.pallas_call(
        functools.partial(_mm_kernel, n_a=len(a.arrays), n_r=len(resid.arrays) if resid is not None else 0,
                          tiles_first=firsts.pop() if firsts else 0, stream=resid is not None),
        grid=(m // tm, n // tn),
        in_specs=in_specs,
        out_specs=out_specs,
        out_shape=out_shape,
        compiler_params=_params("parallel", "arbitrary"),
        name=name,
    )(*args)


def _pool_kernel(up_ref, u_ref, un_ref, w_ref, s_ref, o_ref, *, lay, halo):
    g = pl.program_id(0)
    i = pl.program_id(1)
    tm = lay.tm
    length, pos0 = lay.seq_pos(i)
    x = u_ref[...]
    prev = jnp.where(pos0 == 0, 0.0, up_ref[...])
    nxt = jnp.where(pos0 + tm == length, 0.0, un_ref[...])
    e = jnp.concatenate([prev, x, nxt], axis=0)
    n = tm + 2 * halo
    s1 = e + pltpu.roll(e, 1, 0)
    t = pos0 + lax.broadcasted_iota(jnp.int32, (tm, 1), 0)

    for gi, w in enumerate(POOL_WINDOWS):
        h = w // 2

        @pl.when(g == gi)
        def _(h=h):
            s, k = s1, 1
            while k < h:
                s = pltpu.roll(s, k, 0) + pltpu.roll(s, n - k, 0)
                k *= 2
            win = s[halo:halo + tm]
            cnt = (jnp.minimum(t + h, length) - jnp.maximum(t - h, 0)).astype(F32)
            pooled = win / cnt - x
            y = jnp.dot(pooled.astype(BF16), w_ref[...], preferred_element_type=F32)
            o_ref[...] = (y * s_ref[...]).astype(o_ref.dtype)


def _pool_mixer(u, pool_w, pool_scale, lay):
    n_groups, c, _ = pool_w.shape
    tm, halo = lay.tm, SUBLANES_F32
    assert max(POOL_WINDOWS) // 2 <= halo and len(POOL_WINDOWS) == n_groups
    hb = tm // halo
    last_hb = lay.rows // halo - 1
    return pl.pallas_call(
        functools.partial(_pool_kernel, lay=lay, halo=halo),
        grid=(n_groups, lay.tiles),
        in_specs=[
            pl.BlockSpec((halo, c), lambda g, i: (jnp.maximum(i * hb - 1, 0), g)),
            pl.BlockSpec((tm, c), lambda g, i: (i, g)),
            pl.BlockSpec((halo, c), lambda g, i: (jnp.minimum((i + 1) * hb, last_hb), g)),
            pl.BlockSpec((None, c, c), lambda g, i: (g, 0, 0)),
            pl.BlockSpec((1, c), lambda g, i: (0, g)),
        ],
        out_specs=pl.BlockSpec((tm, c), lambda g, i: (i, g)),
        out_shape=jax.ShapeDtypeStruct((lay.rows, n_groups * c), BF16),
        compiler_params=_params("parallel", "parallel"),
        name="pool_mixer",
    )(u, u, u, pool_w, pool_scale.reshape(1, n_groups * c).astype(F32))


def _chan_dft_kernel(u_ref, w_ref, a_ref, b_ref, *, n):
    ab = jnp.dot(u_ref[...].astype(BF16), w_ref[...], preferred_element_type=F32)
    a_ref[...] = ab[:, :n].astype(a_ref.dtype)
    b_ref[...] = ab[:, n:].astype(b_ref.dtype)


def _chan_dft(u, col0, heads, lay):
    n = FOURIER_HEAD_DIM
    jk = np.outer(np.arange(n), np.arange(n)) % n
    ang = 2.0 * np.pi * jk / n
    w = jnp.asarray(np.concatenate([np.cos(ang), np.sin(ang)], axis=1) / math.sqrt(n), dtype=BF16)
    cb0 = col0 // n
    out = jax.ShapeDtypeStruct((lay.rows, heads * n), BF16)
    return pl.pallas_call(
        functools.partial(_chan_dft_kernel, n=n),
        grid=(heads, lay.tiles),
        in_specs=[pl.BlockSpec((lay.tm, n), lambda h, i: (i, cb0 + h)),
                  pl.BlockSpec((n, 2 * n), lambda h, i: (0, 0))],
        out_specs=[pl.BlockSpec((lay.tm, n), lambda h, i: (i, h))] * 2,
        out_shape=[out, out],
        compiler_params=_params("parallel", "parallel"),
        name="chan_dft",
    )(u, w)


def _trig_rows_kernel(c_ref, s_ref, *, length, mult, tr, tc):
    r = (lax.broadcasted_iota(jnp.int32, (tr, tc), 0) + pl.program_id(0) * tr) * mult
    c = lax.broadcasted_iota(jnp.int32, (tr, tc), 1) + pl.program_id(1) * tc
    ang = ((r * c) & (length - 1)).astype(F32) * (2.0 * math.pi / length)
    c_ref[...] = jnp.cos(ang)
    s_ref[...] = jnp.sin(ang)


def _trig_rows(length, n_rows, mult):
    tr, tc = SUBLANES_F32, min(length, 2048)
    out = jax.ShapeDtypeStruct((n_rows, length), F32)
    return pl.pallas_call(
        functools.partial(_trig_rows_kernel, length=length, mult=mult, tr=tr, tc=tc),
        grid=(n_rows // tr, length // tc),
        out_specs=[pl.BlockSpec((tr, tc), lambda i, j: (i, j))] * 2,
        out_shape=[out, out],
        compiler_params=_params("parallel", "parallel"),
        name="trig_rows",
    )()


def _dft_mat_kernel(ca_ref, sa_ref, cb_ref, sb_ref, c_ref, s_ref, *, split, groups):
    cb, sb = cb_ref[...], sb_ref[...]
    for q in range(groups):
        ca, sa = ca_ref[q:q + 1, :], sa_ref[q:q + 1, :]
        rows = slice(q * split, (q + 1) * split)
        c_ref[rows, :] = (ca * cb - sa * sb).astype(c_ref.dtype)
        s_ref[rows, :] = (-(sa * cb + ca * sb)).astype(s_ref.dtype)


def _dft_matrices(length):
    assert length & (length - 1) == 0
    split = min(DFT_ROW_SPLIT, length // SUBLANES_F32)
    groups = SUBLANES_F32
    ca, sa = _trig_rows(length, length // split, split)
    cb, sb = _trig_rows(length, split, 1)
    tc = min(length, 2048)
    out = jax.ShapeDtypeStruct((length, length), BF16)
    return pl.pallas_call(
        functools.partial(_dft_mat_kernel, split=split, groups=groups),
        grid=(length // (split * groups), length // tc),
        in_specs=[pl.BlockSpec((groups, tc), lambda i, j: (i, j))] * 2
        + [pl.BlockSpec((split, tc), lambda i, j: (0, j))] * 2,
        out_specs=[pl.BlockSpec((split * groups, tc), lambda i, j: (i, j))] * 2,
        out_shape=[out, out],
        compiler_params=_params("parallel", "parallel"),
        name="dft_matrices",
    )(ca, sa, cb, sb)


def _seq_dft_kernel(c_ref, s_ref, a_ref, b_ref, o_ref, acc_ref, *, scale):
    k = pl.program_id(2)

    @pl.when(k == 0)
    def _():
        acc_ref[...] = jnp.zeros_like(acc_ref)

    acc_ref[...] += (jnp.dot(c_ref[...], a_ref[...], preferred_element_type=F32)
                     + jnp.dot(s_ref[...], b_ref[...], preferred_element_type=F32))

    @pl.when(k == pl.num_programs(2) - 1)
    def _():
        o_ref[...] = (acc_ref[...] * scale).astype(o_ref.dtype)


def _seq_dft(va, vb, row0, n_seq, length, tile):
    cmat, nsmat = _dft_matrices(length)
    n = va.shape[1]
    tm = tk = min(tile, length)
    nt = length // tm
    rb0 = row0 // tk
    return pl.pallas_call(
        functools.partial(_seq_dft_kernel, scale=1.0 / math.sqrt(length)),
        grid=(n_seq, nt, nt),
        in_specs=[pl.BlockSpec((tm, tk), lambda s, i, k: (i, k)),
                  pl.BlockSpec((tm, tk), lambda s, i, k: (i, k)),
                  pl.BlockSpec((tk, n), lambda s, i, k: (rb0 + s * nt + k, 0)),
                  pl.BlockSpec((tk, n), lambda s, i, k: (rb0 + s * nt + k, 0))],
        out_specs=pl.BlockSpec((tm, n), lambda s, i, k: (s * nt + i, 0)),
        out_shape=jax.ShapeDtypeStruct((n_seq * length, n), BF16),
        scratch_shapes=[pltpu.VMEM((tm, n), F32)],
        compiler_params=_params("parallel", "parallel", "arbitrary"),
        name="seq_dft",
    )(cmat, nsmat, va, vb)


def _out_proj2_kernel(a_ref, b_ref, w1_ref, w2_ref, *rest, n_r, tiles_first):
    r_refs, outs = rest[:n_r], rest[n_r:]

    def body(r_ref):
        acc = jnp.dot(a_ref[...], w1_ref[...], preferred_element_type=F32)
        acc = acc + jnp.dot(b_ref[...], w2_ref[...], preferred_element_type=F32)
        _emit_stream(acc + r_ref[...], pl.program_id(1), *outs)

    _on_part(pl.program_id(0), tiles_first, [(r,) for r in r_refs], body)


def _out_proj2(a, b, w, resid, *, tm, tn):
    m, ka = a.shape
    kb = b.shape[1]
    n = w.shape[1]
    assert ka % kb == 0
    out_specs, out_shape = _stream_outs(m, n, tm, tn)
    return pl.pallas_call(
        functools.partial(_out_proj2_kernel, n_r=len(resid.arrays), tiles_first=resid.tiles_first(tm)),
        grid=(m // tm, n // tn),
        in_specs=[pl.BlockSpec((tm, ka), lambda i, j: (i, 0)),
                  pl.BlockSpec((tm, kb), lambda i, j: (i, 0)),
                  pl.BlockSpec((ka, tn), lambda i, j: (0, j)),
                  pl.BlockSpec((kb, tn), lambda i, j: (ka // kb, j))] + resid.specs(tm, tn, col=lambda i, j: j),
        out_specs=out_specs,
        out_shape=out_shape,
        compiler_params=_params("parallel", "arbitrary"),
        name="even_out_proj",
    )(a, b, w, w, *resid.arrays)


def _rope_tables_kernel(inv_ref, cos_ref, sa_ref, sb_ref, *, tr, half, length, extra, q_scale):
    row = lax.broadcasted_iota(jnp.int32, (tr, LANES), 0) + pl.program_id(0) * tr
    lane = lax.broadcasted_iota(jnp.int32, (tr, LANES), 1)
    scaled = row >= length + extra
    pos = jnp.where(scaled, row - (length + extra), row)
    ang = pos.astype(F32) * inv_ref[...]
    c, s = jnp.cos(ang), jnp.sin(ang)
    rot = (row < length) | scaled
    scale = jnp.where(scaled, q_scale, 1.0)
    cos_ref[...] = jnp.where(rot & (lane < 2 * half), c, 1.0) * scale
    sa_ref[...] = jnp.where(rot & (lane >= half) & (lane < 2 * half), s, 0.0) * scale
    sb_ref[...] = jnp.where(rot & (lane < half), -s, 0.0) * scale


def _rope_tables(length, head_dim, extra_rows, q_scale):
    assert head_dim == LANES
    rot = head_dim // 4
    half = rot // 2
    inv = np.ones((1, LANES), np.float64)
    freqs = ROPE_THETA ** (-np.arange(0, rot, 2, dtype=np.float64) / rot)
    inv[0, :half] = freqs
    inv[0, half:rot] = freqs
    tr = min(length, 512)
    assert extra_rows % tr == 0
    out = jax.ShapeDtypeStruct((2 * length + extra_rows, LANES), F32)
    return pl.pallas_call(
        functools.partial(_rope_tables_kernel, tr=tr, half=half, length=length, extra=extra_rows, q_scale=q_scale),
        grid=((2 * length + extra_rows) // tr,),
        in_specs=[pl.BlockSpec((1, LANES), lambda i: (0, 0))],
        out_specs=[pl.BlockSpec((tr, LANES), lambda i: (i, 0))] * 3,
        out_shape=[out, out, out],
        compiler_params=_params("parallel"),
        name="rope_tables",
    )(jnp.asarray(inv, dtype=F32))


def _qkv_kernel(h_ref, w_ref, ss_ref, cos_ref, sa_ref, sb_ref, o_ref, y_ref, *, half, dim):
    @pl.when(pl.program_id(0) == 0)
    def _():
        y_ref[...] = jnp.zeros_like(y_ref)

    r = lax.rsqrt(ss_ref[...] * (1.0 / dim) + NORM_EPS)
    cos, sa, sb = cos_ref[...], sa_ref[...], sb_ref[...]
    for c in range(y_ref.shape[1] // LANES):
        yc = y_ref[:, c * LANES:(c + 1) * LANES] * r
        out = yc * cos + pltpu.roll(yc, half, 1) * sa + pltpu.roll(yc, LANES - half, 1) * sb
        o_ref[:, c * LANES:(c + 1) * LANES] = out.astype(o_ref.dtype)
    y_ref[...] = jnp.dot(h_ref[...], w_ref[...], preferred_element_type=F32)


def _qkv_proj(h, ss, w, lay, head_dim, q_scale, *, tn):
    m, k = h.shape
    n = w.shape[1]
    tm = lay.tm
    max_len = max(lay.lp, lay.ls)
    tables = _rope_tables(max_len, head_dim, tm, q_scale)
    q_tiles = (n // 3) // tn
    steps, cur, prev = _lagged_tiles(m // tm, n // tn)

    def pos_block(s):
        i, j = prev(s)
        in_p = i < lay.tiles_p
        blk = jnp.where(in_p, i % (lay.lp // tm), (i - lay.tiles_p) % (lay.ls // tm))
        identity = max_len // tm
        return jnp.where(j < q_tiles, identity + 1 + blk, jnp.where(j < 2 * q_tiles, blk, identity)), 0

    tab = pl.BlockSpec((tm, LANES), pos_block)
    return pl.pallas_call(
        functools.partial(_qkv_kernel, half=head_dim // 8, dim=k),
        grid=(steps + 1,),
        in_specs=[pl.BlockSpec((tm, k), lambda s: (cur(s)[0], 0)),
                  pl.BlockSpec((k, tn), lambda s: (0, cur(s)[1])),
                  pl.BlockSpec((tm, LANES), lambda s: (prev(s)[0], 0)),
                  tab, tab, tab],
        out_specs=pl.BlockSpec((tm, tn), prev),
        out_shape=jax.ShapeDtypeStruct((m, n), BF16),
        scratch_shapes=[pltpu.VMEM((tm, tn), F32)],
        compiler_params=_params("arbitrary"),
        name="qkv_proj",
    )(h, w, ss, *tables)


def _attn_kernel(lq1_ref, lk1_ref, lq2_ref, lk2_ref, q_ref, k_ref, v_ref, w_ref, o_ref, *, hd, lam_init, tk):
    lam = (jnp.exp(jnp.sum(lq1_ref[...] * lk1_ref[...], keepdims=True))
           - jnp.exp(jnp.sum(lq2_ref[...] * lk2_ref[...], keepdims=True)) + lam_init)
    q = q_ref[...]
    qm = (q[:, :hd], q[:, hd:])
    dn = (((1,), (1,)), ((), ()))
    m, l, acc = [None, None], [None, None], [None, None]
    for c in range(k_ref.shape[0] // tk):
        kc = k_ref[c * tk:(c + 1) * tk, :]
        vc = v_ref[c * tk:(c + 1) * tk, :]
        for mp in range(2):
            s = lax.dot_general(qm[mp], kc[:, mp * hd:(mp + 1) * hd], dn, preferred_element_type=F32)
            mc = jnp.max(s, axis=-1, keepdims=True)
            m_new = mc if c == 0 else jnp.maximum(m[mp], mc)
            p = jnp.exp2(s - m_new)
            ls = jnp.sum(p, axis=-1, keepdims=True)
            pv = jnp.dot(p.astype(BF16), vc, preferred_element_type=F32)
            if c == 0:
                l[mp], acc[mp] = ls, pv
            else:
                alpha = jnp.exp2(m[mp] - m_new)
                l[mp] = alpha * l[mp] + ls
                acc[mp] = alpha * acc[mp] + pv
            m[mp] = m_new
    o = acc[0] * (1.0 / l[0]) - acc[1] * (lam / l[1])
    o = o * lax.rsqrt(jnp.mean(o * o, axis=-1, keepdims=True) + SUBLN_EPS)
    o_ref[...] = (o * w_ref[...] * (1.0 - lam_init)).astype(o_ref.dtype)


def _diff_attention(qkv, lams, subln_w, row0, n_seq, length, d_model, hd, layer, *, tq):
    heads = d_model // (2 * hd)
    lam_init = 0.8 - 0.6 * math.exp(-LAMBDA_INIT_DECAY * layer)
    qt = length // tq
    qb0, sb0 = row0 // tq, row0 // length
    vec = pl.BlockSpec((1, hd), lambda b, h, i: (0, 0))
    return pl.pallas_call(
        functools.partial(_attn_kernel, hd=hd, lam_init=lam_init, tk=min(ATTN_KV_CHUNK, length)),
        grid=(n_seq, heads, qt),
        in_specs=[vec, vec, vec, vec,
                  pl.BlockSpec((tq, 2 * hd), lambda b, h, i: (qb0 + b * qt + i, h)),
                  pl.BlockSpec((length, 2 * hd), lambda b, h, i: (sb0 + b, heads + h)),
                  pl.BlockSpec((length, 2 * hd), lambda b, h, i: (sb0 + b, 2 * heads + h)),
                  pl.BlockSpec((1, 2 * hd), lambda b, h, i: (0, 0))],
        out_specs=pl.BlockSpec((tq, 2 * hd), lambda b, h, i: (b * qt + i, h)),
        out_shape=jax.ShapeDtypeStruct((n_seq * length, d_model), BF16),
        compiler_params=_params("parallel", "parallel", "arbitrary"),
        name="diff_attention",
    )(*[v.reshape(1, hd).astype(F32) for v in lams], qkv, qkv, qkv, subln_w.reshape(1, 2 * hd).astype(F32))


def _ffn_up_kernel(hp_ref, h_ref, hn_ref, sp_ref, s_ref, sn_ref, wg_ref, wu0_ref, wu1_ref, cw_ref, cb_ref, g_ref,
                   lhs_ref, r_ref, *, lay, halo, tn, f):
    i, j = pl.program_id(0), pl.program_id(1)
    tm = lay.tm

    @pl.when(j == 0)
    def _():
        length, pos0 = lay.seq_pos(i)
        zero = jnp.zeros(hp_ref.shape, hp_ref.dtype)
        lhs_ref[0:halo, :] = jnp.where(pos0 == 0, zero, hp_ref[...])
        lhs_ref[halo:halo + tm, :] = h_ref[...]
        lhs_ref[halo + tm:, :] = jnp.where(pos0 + tm == length, zero, hn_ref[...])
        inv_dim = 1.0 / h_ref.shape[1]
        r_ref[0:halo, :] = lax.rsqrt(sp_ref[...] * inv_dim + NORM_EPS)
        r_ref[halo:halo + tm, :] = lax.rsqrt(s_ref[...] * inv_dim + NORM_EPS)
        r_ref[halo + tm:, :] = lax.rsqrt(sn_ref[...] * inv_dim + NORM_EPS)

    n = tm + 2 * halo
    lhs = lhs_ref[...]
    cw, cb = cw_ref[...], cb_ref[...]
    r = jnp.concatenate([r_ref[...]] * (tn // LANES), axis=1)

    def conv(u, c0):
        k = slice(c0, c0 + tn)
        u = u * r
        return (pltpu.roll(u, 1, 0)[halo:halo + tm] * cw[0:1, k] + u[halo:halo + tm] * cw[1:2, k]
                + pltpu.roll(u, n - 1, 0)[halo:halo + tm] * cw[2:3, k] + cb[:, k])

    gate = conv(jnp.dot(lhs, wg_ref[...], preferred_element_type=F32), 0)
    up = conv(jnp.concatenate([jnp.dot(lhs, wu0_ref[...], preferred_element_type=F32),
                               jnp.dot(lhs, wu1_ref[...], preferred_element_type=F32)], axis=1), tn)
    g = gate / (1.0 + jnp.exp(-gate)) * up
    col = j * tn + lax.broadcasted_iota(jnp.int32, (1, tn), 1)
    g_ref[...] = jnp.where(col < f, g, 0.0).astype(g_ref.dtype)


def _gate_up_tiles(x, f, f_pad, tn):
    r = x.shape[0]
    x = jnp.pad(x.reshape(r, 2, f), ((0, 0), (0, 0), (0, f_pad - f)))
    return jnp.swapaxes(x.reshape(r, 2, f_pad // tn, tn), 1, 2).reshape(r, 2 * f_pad)


def _ffn_up(h, ss, w_up, layer, conv_w, conv_b, lay, *, tn, f_pad):
    m, k = h.shape
    f = w_up.shape[2] // 2
    half = tn // 2
    assert f % half == 0
    tm, halo = lay.tm, SUBLANES_BF16
    hb = tm // halo
    last_hb = m // halo - 1
    up0, last_half = f // half, 2 * f // half - 1

    def with_halo(width):
        return [pl.BlockSpec((halo, width), lambda i, j: (jnp.maximum(i * hb - 1, 0), 0)),
                pl.BlockSpec((tm, width), lambda i, j: (i, 0)),
                pl.BlockSpec((halo, width), lambda i, j: (jnp.minimum((i + 1) * hb, last_hb), 0))]

    return pl.pallas_call(
        functools.partial(_ffn_up_kernel, lay=lay, halo=halo, tn=tn, f=f),
        grid=(m // tm, f_pad // tn),
        in_specs=with_halo(k) + with_halo(LANES) + [
                  pl.BlockSpec((None, k, tn), lambda i, j: (layer, 0, j)),
                  pl.BlockSpec((None, k, half), lambda i, j: (layer, 0, up0 + 2 * j)),
                  pl.BlockSpec((None, k, half), lambda i, j: (layer, 0, jnp.minimum(up0 + 2 * j + 1, last_half))),
                  pl.BlockSpec((3, 2 * tn), lambda i, j: (0, j)),
                  pl.BlockSpec((1, 2 * tn), lambda i, j: (0, j))],
        out_specs=pl.BlockSpec((tm, tn), lambda i, j: (i, j)),
        out_shape=jax.ShapeDtypeStruct((m, f_pad), BF16),
        scratch_shapes=[pltpu.VMEM((tm + 2 * halo, k), BF16), pltpu.VMEM((tm + 2 * halo, LANES), F32)],
        compiler_params=_params("parallel", "arbitrary"),
        name="ffn_up",
    )(h, h, h, ss, ss, ss, w_up, w_up, w_up,
      _gate_up_tiles(conv_w.astype(F32), f, f_pad, tn), _gate_up_tiles(conv_b.astype(F32).reshape(1, 2 * f), f, f_pad, tn))


def _ffn_down_kernel(g_ref, w_ref, r_ref, o_ref, xb_ref, ss_ref):
    kk = pl.program_id(2)

    @pl.when(kk == 0)
    def _():
        o_ref[...] = r_ref[...]

    o_ref[...] += jnp.dot(g_ref[...], w_ref[...], preferred_element_type=F32)

    @pl.when(kk == pl.num_programs(2) - 1)
    def _():
        _emit_stream(o_ref[...], pl.program_id(1), None, xb_ref, ss_ref)


def _ffn_down(g, w, layer, resid, *, tm, tn, tk):
    m, k = g.shape
    n = w.shape[2]
    return pl.pallas_call(
        _ffn_down_kernel,
        grid=(m // tm, n // tn, k // tk),
        in_specs=[pl.BlockSpec((tm, tk), lambda i, j, kk: (i, kk)),
                  pl.BlockSpec((None, tk, tn), lambda i, j, kk: (layer, kk, j)),
                  pl.BlockSpec((tm, tn), lambda i, j, kk: (i, j))],
        out_specs=[pl.BlockSpec((tm, tn), lambda i, j, kk: (i, j)),
                   pl.BlockSpec((tm, tn), lambda i, j, kk: (i, j)),
                   pl.BlockSpec((tm, LANES), lambda i, j, kk: (i, 0))],
        out_shape=[jax.ShapeDtypeStruct((m, n), F32), jax.ShapeDtypeStruct((m, n), BF16),
                   jax.ShapeDtypeStruct((m, LANES), F32)],
        compiler_params=_params("parallel", "arbitrary", "arbitrary"),
        name="ffn_down",
    )(g, w, resid)


def _down_k_tile(f_pad):
    units = f_pad // MXU_DIM
    best = 1
    for d in range(1, units + 1):
        if units % d == 0 and d * MXU_DIM * 4 <= max(f_pad, 4 * MXU_DIM):
            best = d
    return best * MXU_DIM


def kernel(x_prompt, x_sample, mix_norm_even, w_in_even, pool_w, pool_scale, fourier_w, w_out_even, mix_norm_odd,
           w_qkv, lambda_q1, lambda_k1, lambda_q2, lambda_k2, subln_w, w_out_odd, ffn_norm, w_up, conv_w, conv_b,
           w_down, final_norm):
    bp, lp, d = x_prompt.shape
    bs, ls, _ = x_sample.shape
    assert bp == 1, "the prompt group is handled as one sequence"
    lay = _Layout(bp, lp, bs, ls)
    tm = lay.tm
    depth = ffn_norm.shape[0]
    f = w_down.shape[1]
    tn_ffn = FFN_COL_TILE
    f_pad = -(-f // tn_ffn) * tn_ffn
    hd = lambda_q1.shape[1]
    pool_dim = pool_w.shape[1] * pool_w.shape[2]
    fourier_dim = fourier_w.shape[1]
    col_tile = min(512, d)

    w_up_b = (ffn_norm.astype(F32)[:, :, None] * w_up).astype(BF16)
    w_down_b = jnp.pad(w_down, ((0, 0), (0, f_pad - f), (0, 0))).astype(BF16)

    x = _Rows(x_prompt.reshape(lp, d), x_sample.reshape(bs * ls, d))

    for layer in range(depth):
        i = layer // 2
        if layer % 2 == 0:
            h = _rmsnorm(x, mix_norm_even[i], out_dtype=BF16)
            u = _matmul(_Rows(h), w_in_even[i].astype(BF16), out_dtype=F32, tm=tm, tn=col_tile, name="even_in_proj")
            a = _pool_mixer(u, pool_w[i].astype(BF16), pool_scale[i], lay)
            va, vb = _chan_dft(u, pool_dim, fourier_dim // FOURIER_HEAD_DIM, lay)
            fr = _Rows(_seq_dft(va, vb, 0, bp, lp, MAX_ROW_TILE), _seq_dft(va, vb, lay.rows_p, bs, ls, MAX_ROW_TILE))
            b = _matmul(fr, fourier_w[i].astype(BF16), out_dtype=BF16, tm=tm, tn=col_tile, name="fourier_proj")
            x_new, xb, ss = _out_proj2(a, b, w_out_even[i].astype(BF16), x, tm=tm, tn=col_tile)
        else:
            assert layer > 0, "an odd layer follows a conv-FFN, which supplies bf16(x) and the row sums"
            w_qkv_b = (mix_norm_odd[i].astype(F32)[:, None] * w_qkv[i]).astype(BF16)
            qkv = _qkv_proj(xb, ss, w_qkv_b, lay, hd, hd ** -0.5 * LOG2_E, tn=min(QKV_COL_TILE, d))
            lams = (lambda_q1[i], lambda_k1[i], lambda_q2[i], lambda_k2[i])
            o = _Rows(_diff_attention(qkv, lams, subln_w[i], 0, bp, lp, d, hd, layer, tq=min(ATTN_Q_TILE_LONG, lp)),
                      _diff_attention(qkv, lams, subln_w[i], lay.rows_p, bs, ls, d, hd, layer,
                                      tq=min(ATTN_Q_TILE_SHORT, ls)))
            x_new, xb, ss = _matmul(o, w_out_odd[i].astype(BF16), resid=x, tm=tm, tn=col_tile, name="odd_out_proj")

        g = _ffn_up(xb, ss, w_up_b, layer, conv_w[layer], conv_b[layer], lay, tn=tn_ffn, f_pad=f_pad)
        x_new, xb, ss = _ffn_down(g, w_down_b, layer, x_new, tm=tm, tn=min(1024, d), tk=_down_k_tile(f_pad))
        x = _Rows(x_new)

    y_prompt = _rmsnorm(x, final_norm, out_dtype=F32, row0=0, rows=lay.rows_p)
    y_sample = _rmsnorm(x, final_norm, out_dtype=F32, row0=lay.rows_p, rows=lay.rows_s)
    return y_prompt.reshape(bp, lp, d), y_sample.reshape(bs, ls, d)
```

```python
import functools
import math

import numpy as np
import jax
import jax.numpy as jnp
from jax import lax
from jax.experimental import pallas as pl
from jax.experimental.pallas import tpu as pltpu

F32 = jnp.float32
BF16 = jnp.bfloat16

POOL_WINDOWS = (2, 4, 8, 16)
FOURIER_HEAD_DIM = 256
ROPE_THETA = 500000.0
NORM_EPS = 1e-6
SUBLN_EPS = 1e-5
LAMBDA_INIT_DECAY = 0.3

LANES = 128
SUBLANES_F32 = 8
SUBLANES_BF16 = 16
MXU_DIM = 256
VMEM_LIMIT_BYTES = 56 * 1024 * 1024

MAX_ROW_TILE = 1024
NORM_ROW_TILE = 256
FFN_COL_TILE = 512
QKV_COL_TILE = 1024
ATTN_TILE_LONG = (512, 1024)
ATTN_TILE_SHORT = (2048, 512)
DFT_ROW_SPLIT = 64
LOG2_E = math.log2(math.e)


def _params(*sem):
    return pltpu.CompilerParams(dimension_semantics=sem, vmem_limit_bytes=VMEM_LIMIT_BYTES)


class _Layout:
    def __init__(self, n_prompt, len_prompt, n_sample, len_sample):
        self.lp, self.ls = len_prompt, len_sample
        self.rows_p, self.rows_s = n_prompt * len_prompt, n_sample * len_sample
        self.rows = self.rows_p + self.rows_s
        self.tm = min(MAX_ROW_TILE, len_prompt, len_sample)
        assert len_prompt % self.tm == 0 and len_sample % self.tm == 0
        self.tiles_p = self.rows_p // self.tm
        self.tiles = self.rows // self.tm

    def seq_pos(self, i):
        in_p = i < self.tiles_p
        length = jnp.where(in_p, self.lp, self.ls)
        tile_in_seq = jnp.where(in_p, i % (self.lp // self.tm), (i - self.tiles_p) % (self.ls // self.tm))
        return length, tile_in_seq * self.tm


class _Rows:
    def __init__(self, *arrays):
        assert 1 <= len(arrays) <= 2
        self.arrays = arrays
        self.first_rows = arrays[0].shape[0]
        self.rows = sum(a.shape[0] for a in arrays)
        self.width = arrays[0].shape[1]

    def tiles_first(self, tm):
        assert self.first_rows % tm == 0
        return self.first_rows // tm

    def specs(self, tm, block_width, row=lambda *idx: idx[0], col=lambda *idx: 0):
        shape = (tm, block_width)
        if len(self.arrays) == 1:
            return [pl.BlockSpec(shape, lambda *idx: (row(*idx), col(*idx)))]
        tp = self.tiles_first(tm)
        return [pl.BlockSpec(shape, lambda *idx: (jnp.minimum(row(*idx), tp - 1), col(*idx))),
                pl.BlockSpec(shape, lambda *idx: (jnp.maximum(row(*idx) - tp, 0), col(*idx)))]


def _on_part(i, tiles_first, ref_groups, body):
    if len(ref_groups) == 1:
        body(*ref_groups[0])
        return
    pl.when(i < tiles_first)(lambda: body(*ref_groups[0]))
    pl.when(i >= tiles_first)(lambda: body(*ref_groups[1]))


def _lagged_tiles(n_i, n_j):
    n = n_i * n_j

    def cur(s):
        c = jnp.minimum(s, n - 1)
        return c // n_j, c % n_j

    def prev(s):
        p = jnp.maximum(s - 1, 0)
        return p // n_j, p % n_j

    return n, cur, prev


def _rmsnorm_kernel(*refs, eps, n_src, tiles_first):
    g_ref, o_ref = refs[n_src], refs[n_src + 1]

    def body(x_ref):
        x = x_ref[...]
        y = x * lax.rsqrt(jnp.mean(x * x, axis=-1, keepdims=True) + eps)
        o_ref[...] = (y * g_ref[...]).astype(o_ref.dtype)

    _on_part(pl.program_id(0), tiles_first, [(r,) for r in refs[:n_src]], body)


def _rmsnorm(src, g, *, out_dtype, row0=0, rows=None):
    d = src.width
    tm = NORM_ROW_TILE
    rows = src.rows - row0 if rows is None else rows
    rb0 = row0 // tm
    n_src = len(src.arrays)
    return pl.pallas_call(
        functools.partial(_rmsnorm_kernel, eps=NORM_EPS, n_src=n_src, tiles_first=src.tiles_first(tm)),
        grid=(rows // tm,),
        in_specs=src.specs(tm, d, row=lambda i: i + rb0) + [pl.BlockSpec((1, d), lambda i: (0, 0))],
        out_specs=pl.BlockSpec((tm, d), lambda i: (i, 0)),
        out_shape=jax.ShapeDtypeStruct((rows, d), out_dtype),
        compiler_params=_params("parallel"),
        name="rmsnorm",
    )(*src.arrays, g.reshape(1, d).astype(F32))


def _emit_stream(x, j, o_ref, xb_ref, ss_ref):
    if o_ref is not None:
        o_ref[...] = x
    xb_ref[...] = x.astype(xb_ref.dtype)
    part = jnp.broadcast_to(jnp.sum(x * x, axis=-1, keepdims=True), ss_ref.shape)

    @pl.when(j == 0)
    def _():
        ss_ref[...] = part

    @pl.when(j > 0)
    def _():
        ss_ref[...] += part


def _stream_outs(m, n, tm, tn):
    specs = [pl.BlockSpec((tm, tn), lambda i, j: (i, j)),
             pl.BlockSpec((tm, tn), lambda i, j: (i, j)),
             pl.BlockSpec((tm, LANES), lambda i, j: (i, 0))]
    shapes = [jax.ShapeDtypeStruct((m, n), F32), jax.ShapeDtypeStruct((m, n), BF16),
              jax.ShapeDtypeStruct((m, LANES), F32)]
    return specs, shapes


def _mm_kernel(*refs, n_a, n_r, tiles_first, stream):
    a_refs, b_ref = refs[:n_a], refs[n_a]
    r_refs = refs[n_a + 1:n_a + 1 + n_r]
    outs = refs[n_a + 1 + n_r:]
    i, j = pl.program_id(0), pl.program_id(1)

    def body(a_ref, r_ref):
        acc = jnp.dot(a_ref[...], b_ref[...], preferred_element_type=F32)
        if r_ref is not None:
            acc = acc + r_ref[...]
        if stream:
            _emit_stream(acc, j, *outs)
        else:
            outs[0][...] = acc.astype(outs[0].dtype)

    n_parts = max(n_a, n_r)
    groups = [(a_refs[min(p, n_a - 1)], r_refs[min(p, n_r - 1)] if n_r else None) for p in range(n_parts)]
    _on_part(i, tiles_first, groups, body)


def _matmul(a, b, *, tm, tn, name, out_dtype=None, resid=None):
    m, k = a.rows, a.width
    n = b.shape[1]
    srcs = [a] + ([resid] if resid is not None else [])
    firsts = {s.tiles_first(tm) for s in srcs if len(s.arrays) == 2}
    assert len(firsts) <= 1, "two-part operands must split at the same row"
    in_specs = a.specs(tm, k) + [pl.BlockSpec((k, tn), lambda i, j: (0, j))]
    args = list(a.arrays) + [b]
    if resid is not None:
        in_specs += resid.specs(tm, tn, col=lambda i, j: j)
        args += list(resid.arrays)
        out_specs, out_shape = _stream_outs(m, n, tm, tn)
    else:
        out_specs = pl.BlockSpec((tm, tn), lambda i, j: (i, j))
        out_shape = jax.ShapeDtypeStruct((m, n), out_dtype)
    return pl.pallas_call(
        functools.partial(_mm_kernel, n_a=len(a.arrays), n_r=len(resid.arrays) if resid is not None else 0,
                          tiles_first=firsts.pop() if firsts else 0, stream=resid is not None),
        grid=(m // tm, n // tn),
        in_specs=in_specs,
        out_specs=out_specs,
        out_shape=out_shape,
        compiler_params=_params("parallel", "arbitrary"),
        name=name,
    )(*args)


def _pool_kernel(up_ref, u_ref, un_ref, w_ref, s_ref, o_ref, *, lay, halo):
    g = pl.program_id(0)
    i = pl.program_id(1)
    tm = lay.tm
    length, pos0 = lay.seq_pos(i)
    x = u_ref[...]
    prev = jnp.where(pos0 == 0, 0.0, up_ref[...])
    nxt = jnp.where(pos0 + tm == length, 0.0, un_ref[...])
    e = jnp.concatenate([prev, x, nxt], axis=0)
    n = tm + 2 * halo
    s1 = e + pltpu.roll(e, 1, 0)
    t = pos0 + lax.broadcasted_iota(jnp.int32, (tm, 1), 0)

    for gi, w in enumerate(POOL_WINDOWS):
        h = w // 2

        @pl.when(g == gi)
        def _(h=h):
            s, k = s1, 1
            while k < h:
                s = pltpu.roll(s, k, 0) + pltpu.roll(s, n - k, 0)
                k *= 2
            win = s[halo:halo + tm]
            cnt = (jnp.minimum(t + h, length) - jnp.maximum(t - h, 0)).astype(F32)
            pooled = win / cnt - x
            y = jnp.dot(pooled.astype(BF16), w_ref[...], preferred_element_type=F32)
            o_ref[...] = (y * s_ref[...]).astype(o_ref.dtype)


def _pool_mixer(u, pool_w, pool_scale, lay):
    n_groups, c, _ = pool_w.shape
    tm, halo = lay.tm, SUBLANES_F32
    assert max(POOL_WINDOWS) // 2 <= halo and len(POOL_WINDOWS) == n_groups
    hb = tm // halo
    last_hb = lay.rows // halo - 1
    return pl.pallas_call(
        functools.partial(_pool_kernel, lay=lay, halo=halo),
        grid=(n_groups, lay.tiles),
        in_specs=[
            pl.BlockSpec((halo, c), lambda g, i: (jnp.maximum(i * hb - 1, 0), g)),
            pl.BlockSpec((tm, c), lambda g, i: (i, g)),
            pl.BlockSpec((halo, c), lambda g, i: (jnp.minimum((i + 1) * hb, last_hb), g)),
            pl.BlockSpec((None, c, c), lambda g, i: (g, 0, 0)),
            pl.BlockSpec((1, c), lambda g, i: (0, g)),
        ],
        out_specs=pl.BlockSpec((tm, c), lambda g, i: (i, g)),
        out_shape=jax.ShapeDtypeStruct((lay.rows, n_groups * c), BF16),
        compiler_params=_params("parallel", "parallel"),
        name="pool_mixer",
    )(u, u, u, pool_w, pool_scale.reshape(1, n_groups * c).astype(F32))


def _chan_dft_kernel(u_ref, w_ref, a_ref, b_ref, *, n):
    ab = jnp.dot(u_ref[...].astype(BF16), w_ref[...], preferred_element_type=F32)
    a_ref[...] = ab[:, :n].astype(a_ref.dtype)
    b_ref[...] = ab[:, n:].astype(b_ref.dtype)


def _chan_dft(u, col0, heads, lay):
    n = FOURIER_HEAD_DIM
    jk = np.outer(np.arange(n), np.arange(n)) % n
    ang = 2.0 * np.pi * jk / n
    w = jnp.asarray(np.concatenate([np.cos(ang), np.sin(ang)], axis=1) / math.sqrt(n), dtype=BF16)
    cb0 = col0 // n
    out = jax.ShapeDtypeStruct((lay.rows, heads * n), BF16)
    return pl.pallas_call(
        functools.partial(_chan_dft_kernel, n=n),
        grid=(heads, lay.tiles),
        in_specs=[pl.BlockSpec((lay.tm, n), lambda h, i: (i, cb0 + h)),
                  pl.BlockSpec((n, 2 * n), lambda h, i: (0, 0))],
        out_specs=[pl.BlockSpec((lay.tm, n), lambda h, i: (i, h))] * 2,
        out_shape=[out, out],
        compiler_params=_params("parallel", "parallel"),
        name="chan_dft",
    )(u, w)


def _trig_rows_kernel(c_ref, s_ref, *, length, mult, tr, tc):
    r = (lax.broadcasted_iota(jnp.int32, (tr, tc), 0) + pl.program_id(0) * tr) * mult
    c = lax.broadcasted_iota(jnp.int32, (tr, tc), 1) + pl.program_id(1) * tc
    ang = ((r * c) & (length - 1)).astype(F32) * (2.0 * math.pi / length)
    c_ref[...] = jnp.cos(ang)
    s_ref[...] = jnp.sin(ang)


def _trig_rows(length, n_rows, mult):
    tr, tc = SUBLANES_F32, min(length, 2048)
    out = jax.ShapeDtypeStruct((n_rows, length), F32)
    return pl.pallas_call(
        functools.partial(_trig_rows_kernel, length=length, mult=mult, tr=tr, tc=tc),
        grid=(n_rows // tr, length // tc),
        out_specs=[pl.BlockSpec((tr, tc), lambda i, j: (i, j))] * 2,
        out_shape=[out, out],
        compiler_params=_params("parallel", "parallel"),
        name="trig_rows",
    )()


def _dft_mat_kernel(ca_ref, sa_ref, cb_ref, sb_ref, c_ref, s_ref, *, split, groups):
    cb, sb = cb_ref[...], sb_ref[...]
    for q in range(groups):
        ca, sa = ca_ref[q:q + 1, :], sa_ref[q:q + 1, :]
        rows = slice(q * split, (q + 1) * split)
        c_ref[rows, :] = (ca * cb - sa * sb).astype(c_ref.dtype)
        s_ref[rows, :] = (-(sa * cb + ca * sb)).astype(s_ref.dtype)


def _dft_matrices(length):
    assert length & (length - 1) == 0
    split = min(DFT_ROW_SPLIT, length // SUBLANES_F32)
    groups = SUBLANES_F32
    ca, sa = _trig_rows(length, length // split, split)
    cb, sb = _trig_rows(length, split, 1)
    tc = min(length, 2048)
    out = jax.ShapeDtypeStruct((length, length), BF16)
    return pl.pallas_call(
        functools.partial(_dft_mat_kernel, split=split, groups=groups),
        grid=(length // (split * groups), length // tc),
        in_specs=[pl.BlockSpec((groups, tc), lambda i, j: (i, j))] * 2
        + [pl.BlockSpec((split, tc), lambda i, j: (0, j))] * 2,
        out_specs=[pl.BlockSpec((split * groups, tc), lambda i, j: (i, j))] * 2,
        out_shape=[out, out],
        compiler_params=_params("parallel", "parallel"),
        name="dft_matrices",
    )(ca, sa, cb, sb)


def _seq_dft_kernel(c_ref, s_ref, a_ref, b_ref, o_ref, acc_ref, *, scale):
    k = pl.program_id(2)

    @pl.when(k == 0)
    def _():
        acc_ref[...] = jnp.zeros_like(acc_ref)

    acc_ref[...] += (jnp.dot(c_ref[...], a_ref[...], preferred_element_type=F32)
                     + jnp.dot(s_ref[...], b_ref[...], preferred_element_type=F32))

    @pl.when(k == pl.num_programs(2) - 1)
    def _():
        o_ref[...] = (acc_ref[...] * scale).astype(o_ref.dtype)


def _seq_dft(va, vb, row0, n_seq, length, tile):
    cmat, nsmat = _dft_matrices(length)
    n = va.shape[1]
    tm = tk = min(tile, length)
    nt = length // tm
    rb0 = row0 // tk
    return pl.pallas_call(
        functools.partial(_seq_dft_kernel, scale=1.0 / math.sqrt(length)),
        grid=(n_seq, nt, nt),
        in_specs=[pl.BlockSpec((tm, tk), lambda s, i, k: (i, k)),
                  pl.BlockSpec((tm, tk), lambda s, i, k: (i, k)),
                  pl.BlockSpec((tk, n), lambda s, i, k: (rb0 + s * nt + k, 0)),
                  pl.BlockSpec((tk, n), lambda s, i, k: (rb0 + s * nt + k, 0))],
        out_specs=pl.BlockSpec((tm, n), lambda s, i, k: (s * nt + i, 0)),
        out_shape=jax.ShapeDtypeStruct((n_seq * length, n), BF16),
        scratch_shapes=[pltpu.VMEM((tm, n), F32)],
        compiler_params=_params("parallel", "parallel", "arbitrary"),
        name="seq_dft",
    )(cmat, nsmat, va, vb)


def _out_proj2_kernel(a_ref, b_ref, w1_ref, w2_ref, *rest, n_r, tiles_first):
    r_refs, outs = rest[:n_r], rest[n_r:]

    def body(r_ref):
        acc = jnp.dot(a_ref[...], w1_ref[...], preferred_element_type=F32)
        acc = acc + jnp.dot(b_ref[...], w2_ref[...], preferred_element_type=F32)
        _emit_stream(acc + r_ref[...], pl.program_id(1), *outs)

    _on_part(pl.program_id(0), tiles_first, [(r,) for r in r_refs], body)


def _out_proj2(a, b, w, resid, *, tm, tn):
    m, ka = a.shape
    kb = b.shape[1]
    n = w.shape[1]
    assert ka % kb == 0
    out_specs, out_shape = _stream_outs(m, n, tm, tn)
    return pl.pallas_call(
        functools.partial(_out_proj2_kernel, n_r=len(resid.arrays), tiles_first=resid.tiles_first(tm)),
        grid=(m // tm, n // tn),
        in_specs=[pl.BlockSpec((tm, ka), lambda i, j: (i, 0)),
                  pl.BlockSpec((tm, kb), lambda i, j: (i, 0)),
                  pl.BlockSpec((ka, tn), lambda i, j: (0, j)),
                  pl.BlockSpec((kb, tn), lambda i, j: (ka // kb, j))] + resid.specs(tm, tn, col=lambda i, j: j),
        out_specs=out_specs,
        out_shape=out_shape,
        compiler_params=_params("parallel", "arbitrary"),
        name="even_out_proj",
    )(a, b, w, w, *resid.arrays)


def _rope_tables_kernel(inv_ref, cos_ref, sa_ref, sb_ref, *, tr, half, length, extra, q_scale):
    row = lax.broadcasted_iota(jnp.int32, (tr, LANES), 0) + pl.program_id(0) * tr
    lane = lax.broadcasted_iota(jnp.int32, (tr, LANES), 1)
    scaled = row >= length + extra
    pos = jnp.where(scaled, row - (length + extra), row)
    ang = pos.astype(F32) * inv_ref[...]
    c, s = jnp.cos(ang), jnp.sin(ang)
    rot = (row < length) | scaled
    scale = jnp.where(scaled, q_scale, 1.0)
    cos_ref[...] = jnp.where(rot & (lane < 2 * half), c, 1.0) * scale
    sa_ref[...] = jnp.where(rot & (lane >= half) & (lane < 2 * half), s, 0.0) * scale
    sb_ref[...] = jnp.where(rot & (lane < half), -s, 0.0) * scale


def _rope_tables(length, head_dim, extra_rows, q_scale):
    assert head_dim == LANES
    rot = head_dim // 4
    half = rot // 2
    inv = np.ones((1, LANES), np.float64)
    freqs = ROPE_THETA ** (-np.arange(0, rot, 2, dtype=np.float64) / rot)
    inv[0, :half] = freqs
    inv[0, half:rot] = freqs
    tr = min(length, 512)
    assert extra_rows % tr == 0
    out = jax.ShapeDtypeStruct((2 * length + extra_rows, LANES), F32)
    return pl.pallas_call(
        functools.partial(_rope_tables_kernel, tr=tr, half=half, length=length, extra=extra_rows, q_scale=q_scale),
        grid=((2 * length + extra_rows) // tr,),
        in_specs=[pl.BlockSpec((1, LANES), lambda i: (0, 0))],
        out_specs=[pl.BlockSpec((tr, LANES), lambda i: (i, 0))] * 3,
        out_shape=[out, out, out],
        compiler_params=_params("parallel"),
        name="rope_tables",
    )(jnp.asarray(inv, dtype=F32))


def _qkv_kernel(h_ref, w_ref, ss_ref, cos_ref, sa_ref, sb_ref, o_ref, y_ref, *, half, dim):
    @pl.when(pl.program_id(0) == 0)
    def _():
        y_ref[...] = jnp.zeros_like(y_ref)

    r = lax.rsqrt(ss_ref[...] * (1.0 / dim) + NORM_EPS)
    cos, sa, sb = cos_ref[...], sa_ref[...], sb_ref[...]
    for c in range(y_ref.shape[1] // LANES):
        yc = y_ref[:, c * LANES:(c + 1) * LANES] * r
        out = yc * cos + pltpu.roll(yc, half, 1) * sa + pltpu.roll(yc, LANES - half, 1) * sb
        o_ref[:, c * LANES:(c + 1) * LANES] = out.astype(o_ref.dtype)
    y_ref[...] = jnp.dot(h_ref[...], w_ref[...], preferred_element_type=F32)


def _qkv_proj(h, ss, w, lay, head_dim, q_scale, *, tn):
    m, k = h.shape
    n = w.shape[1]
    tm = lay.tm
    max_len = max(lay.lp, lay.ls)
    tables = _rope_tables(max_len, head_dim, tm, q_scale)
    q_tiles = (n // 3) // tn
    steps, cur, prev = _lagged_tiles(m // tm, n // tn)

    def pos_block(s):
        i, j = prev(s)
        in_p = i < lay.tiles_p
        blk = jnp.where(in_p, i % (lay.lp // tm), (i - lay.tiles_p) % (lay.ls // tm))
        identity = max_len // tm
        return jnp.where(j < q_tiles, identity + 1 + blk, jnp.where(j < 2 * q_tiles, blk, identity)), 0

    tab = pl.BlockSpec((tm, LANES), pos_block)
    return pl.pallas_call(
        functools.partial(_qkv_kernel, half=head_dim // 8, dim=k),
        grid=(steps + 1,),
        in_specs=[pl.BlockSpec((tm, k), lambda s: (cur(s)[0], 0)),
                  pl.BlockSpec((k, tn), lambda s: (0, cur(s)[1])),
                  pl.BlockSpec((tm, LANES), lambda s: (prev(s)[0], 0)),
                  tab, tab, tab],
        out_specs=pl.BlockSpec((tm, tn), prev),
        out_shape=jax.ShapeDtypeStruct((m, n), BF16),
        scratch_shapes=[pltpu.VMEM((tm, tn), F32)],
        compiler_params=_params("arbitrary"),
        name="qkv_proj",
    )(h, w, ss, *tables)


def _attn_kernel(lq1_ref, lk1_ref, lq2_ref, lk2_ref, q_ref, k_ref, v_ref, w_ref, o_ref, *, hd, lam_init, tk):
    lam = (jnp.exp(jnp.sum(lq1_ref[...] * lk1_ref[...], keepdims=True))
           - jnp.exp(jnp.sum(lq2_ref[...] * lk2_ref[...], keepdims=True)) + lam_init)
    q = q_ref[...]
    qm = (q[:, :hd], q[:, hd:])
    dn = (((1,), (1,)), ((), ()))
    m, l, acc = [None, None], [None, None], [None, None]
    for c in range(k_ref.shape[0] // tk):
        kc = k_ref[c * tk:(c + 1) * tk, :]
        vc = v_ref[c * tk:(c + 1) * tk, :]
        for mp in range(2):
            s = lax.dot_general(qm[mp], kc[:, mp * hd:(mp + 1) * hd], dn, preferred_element_type=F32)
            mc = jnp.max(s, axis=-1, keepdims=True)
            m_new = mc if c == 0 else jnp.maximum(m[mp], mc)
            p = jnp.exp2(s - m_new)
            ls = jnp.sum(p, axis=-1, keepdims=True)
            pv = jnp.dot(p.astype(BF16), vc, preferred_element_type=F32)
            if c == 0:
                l[mp], acc[mp] = ls, pv
            else:
                alpha = jnp.exp2(m[mp] - m_new)
                l[mp] = alpha * l[mp] + ls
                acc[mp] = alpha * acc[mp] + pv
            m[mp] = m_new
    o = acc[0] * (1.0 / l[0]) - acc[1] * (lam / l[1])
    o = o * lax.rsqrt(jnp.mean(o * o, axis=-1, keepdims=True) + SUBLN_EPS)
    o_ref[...] = (o * w_ref[...] * (1.0 - lam_init)).astype(o_ref.dtype)


def _diff_attention(qkv, lams, subln_w, row0, n_seq, length, d_model, hd, layer, tile):
    tq, tk = (min(t, length) for t in tile)
    heads = d_model // (2 * hd)
    lam_init = 0.8 - 0.6 * math.exp(-LAMBDA_INIT_DECAY * layer)
    qt = length // tq
    qb0, sb0 = row0 // tq, row0 // length
    vec = pl.BlockSpec((1, hd), lambda b, h, i: (0, 0))
    return pl.pallas_call(
        functools.partial(_attn_kernel, hd=hd, lam_init=lam_init, tk=tk),
        grid=(n_seq, heads, qt),
        in_specs=[vec, vec, vec, vec,
                  pl.BlockSpec((tq, 2 * hd), lambda b, h, i: (qb0 + b * qt + i, h)),
                  pl.BlockSpec((length, 2 * hd), lambda b, h, i: (sb0 + b, heads + h)),
                  pl.BlockSpec((length, 2 * hd), lambda b, h, i: (sb0 + b, 2 * heads + h)),
                  pl.BlockSpec((1, 2 * hd), lambda b, h, i: (0, 0))],
        out_specs=pl.BlockSpec((tq, 2 * hd), lambda b, h, i: (b * qt + i, h)),
        out_shape=jax.ShapeDtypeStruct((n_seq * length, d_model), BF16),
        compiler_params=_params("parallel", "parallel", "arbitrary"),
        name="diff_attention",
    )(*[v.reshape(1, hd).astype(F32) for v in lams], qkv, qkv, qkv, subln_w.reshape(1, 2 * hd).astype(F32))


def _ffn_up_kernel(hp_ref, h_ref, hn_ref, sp_ref, s_ref, sn_ref, wg_ref, wu0_ref, wu1_ref, cw_ref, cb_ref, g_ref,
                   lhs_ref, r_ref, *, lay, halo, tn, f):
    i, j = pl.program_id(0), pl.program_id(1)
    tm = lay.tm

    @pl.when(j == 0)
    def _():
        length, pos0 = lay.seq_pos(i)
        zero = jnp.zeros(hp_ref.shape, hp_ref.dtype)
        lhs_ref[0:halo, :] = jnp.where(pos0 == 0, zero, hp_ref[...])
        lhs_ref[halo:halo + tm, :] = h_ref[...]
        lhs_ref[halo + tm:, :] = jnp.where(pos0 + tm == length, zero, hn_ref[...])
        inv_dim = 1.0 / h_ref.shape[1]
        r_ref[0:halo, :] = lax.rsqrt(sp_ref[...] * inv_dim + NORM_EPS)
        r_ref[halo:halo + tm, :] = lax.rsqrt(s_ref[...] * inv_dim + NORM_EPS)
        r_ref[halo + tm:, :] = lax.rsqrt(sn_ref[...] * inv_dim + NORM_EPS)

    n = tm + 2 * halo
    lhs = lhs_ref[...]
    cw, cb = cw_ref[...], cb_ref[...]
    r = jnp.concatenate([r_ref[...]] * (tn // LANES), axis=1)

    def conv(u, c0):
        k = slice(c0, c0 + tn)
        u = u * r
        return (pltpu.roll(u, 1, 0)[halo:halo + tm] * cw[0:1, k] + u[halo:halo + tm] * cw[1:2, k]
                + pltpu.roll(u, n - 1, 0)[halo:halo + tm] * cw[2:3, k] + cb[:, k])

    gate = conv(jnp.dot(lhs, wg_ref[...], preferred_element_type=F32), 0)
    up = conv(jnp.concatenate([jnp.dot(lhs, wu0_ref[...], preferred_element_type=F32),
                               jnp.dot(lhs, wu1_ref[...], preferred_element_type=F32)], axis=1), tn)
    g = gate / (1.0 + jnp.exp(-gate)) * up
    col = j * tn + lax.broadcasted_iota(jnp.int32, (1, tn), 1)
    g_ref[...] = jnp.where(col < f, g, 0.0).astype(g_ref.dtype)


def _gate_up_tiles(x, f, f_pad, tn):
    r = x.shape[0]
    x = jnp.pad(x.reshape(r, 2, f), ((0, 0), (0, 0), (0, f_pad - f)))
    return jnp.swapaxes(x.reshape(r, 2, f_pad // tn, tn), 1, 2).reshape(r, 2 * f_pad)


def _ffn_up(h, ss, w_up, layer, conv_w, conv_b, lay, *, tn, f_pad):
    m, k = h.shape
    f = w_up.shape[2] // 2
    half = tn // 2
    assert f % half == 0
    tm, halo = lay.tm, SUBLANES_BF16
    hb = tm // halo
    last_hb = m // halo - 1
    up0, last_half = f // half, 2 * f // half - 1

    def with_halo(width):
        return [pl.BlockSpec((halo, width), lambda i, j: (jnp.maximum(i * hb - 1, 0), 0)),
                pl.BlockSpec((tm, width), lambda i, j: (i, 0)),
                pl.BlockSpec((halo, width), lambda i, j: (jnp.minimum((i + 1) * hb, last_hb), 0))]

    return pl.pallas_call(
        functools.partial(_ffn_up_kernel, lay=lay, halo=halo, tn=tn, f=f),
        grid=(m // tm, f_pad // tn),
        in_specs=with_halo(k) + with_halo(LANES) + [
                  pl.BlockSpec((None, k, tn), lambda i, j: (layer, 0, j)),
                  pl.BlockSpec((None, k, half), lambda i, j: (layer, 0, up0 + 2 * j)),
                  pl.BlockSpec((None, k, half), lambda i, j: (layer, 0, jnp.minimum(up0 + 2 * j + 1, last_half))),
                  pl.BlockSpec((3, 2 * tn), lambda i, j: (0, j)),
                  pl.BlockSpec((1, 2 * tn), lambda i, j: (0, j))],
        out_specs=pl.BlockSpec((tm, tn), lambda i, j: (i, j)),
        out_shape=jax.ShapeDtypeStruct((m, f_pad), BF16),
        scratch_shapes=[pltpu.VMEM((tm + 2 * halo, k), BF16), pltpu.VMEM((tm + 2 * halo, LANES), F32)],
        compiler_params=_params("parallel", "arbitrary"),
        name="ffn_up",
    )(h, h, h, ss, ss, ss, w_up, w_up, w_up,
      _gate_up_tiles(conv_w.astype(F32), f, f_pad, tn), _gate_up_tiles(conv_b.astype(F32).reshape(1, 2 * f), f, f_pad, tn))


def _ffn_down_kernel(g_ref, w_ref, r_ref, o_ref, xb_ref, ss_ref):
    kk = pl.program_id(2)

    @pl.when(kk == 0)
    def _():
        o_ref[...] = r_ref[...]

    o_ref[...] += jnp.dot(g_ref[...], w_ref[...], preferred_element_type=F32)

    @pl.when(kk == pl.num_programs(2) - 1)
    def _():
        _emit_stream(o_ref[...], pl.program_id(1), None, xb_ref, ss_ref)


def _ffn_down(g, w, layer, resid, *, tm, tn, tk):
    m, k = g.shape
    n = w.shape[2]
    return pl.pallas_call(
        _ffn_down_kernel,
        grid=(m // tm, n // tn, k // tk),
        in_specs=[pl.BlockSpec((tm, tk), lambda i, j, kk: (i, kk)),
                  pl.BlockSpec((None, tk, tn), lambda i, j, kk: (layer, kk, j)),
                  pl.BlockSpec((tm, tn), lambda i, j, kk: (i, j))],
        out_specs=[pl.BlockSpec((tm, tn), lambda i, j, kk: (i, j)),
                   pl.BlockSpec((tm, tn), lambda i, j, kk: (i, j)),
                   pl.BlockSpec((tm, LANES), lambda i, j, kk: (i, 0))],
        out_shape=[jax.ShapeDtypeStruct((m, n), F32), jax.ShapeDtypeStruct((m, n), BF16),
                   jax.ShapeDtypeStruct((m, LANES), F32)],
        compiler_params=_params("parallel", "arbitrary", "arbitrary"),
        name="ffn_down",
    )(g, w, resid)


def _down_k_tile(f_pad):
    units = f_pad // MXU_DIM
    best = 1
    for d in range(1, units + 1):
        if units % d == 0 and d * MXU_DIM * 4 <= max(f_pad, 4 * MXU_DIM):
            best = d
    return best * MXU_DIM


def kernel(x_prompt, x_sample, mix_norm_even, w_in_even, pool_w, pool_scale, fourier_w, w_out_even, mix_norm_odd,
           w_qkv, lambda_q1, lambda_k1, lambda_q2, lambda_k2, subln_w, w_out_odd, ffn_norm, w_up, conv_w, conv_b,
           w_down, final_norm):
    bp, lp, d = x_prompt.shape
    bs, ls, _ = x_sample.shape
    assert bp == 1, "the prompt group is handled as one sequence"
    lay = _Layout(bp, lp, bs, ls)
    tm = lay.tm
    depth = ffn_norm.shape[0]
    f = w_down.shape[1]
    tn_ffn = FFN_COL_TILE
    f_pad = -(-f // tn_ffn) * tn_ffn
    hd = lambda_q1.shape[1]
    pool_dim = pool_w.shape[1] * pool_w.shape[2]
    fourier_dim = fourier_w.shape[1]
    col_tile = min(512, d)

    w_up_b = (ffn_norm.astype(F32)[:, :, None] * w_up).astype(BF16)
    w_down_b = jnp.pad(w_down, ((0, 0), (0, f_pad - f), (0, 0))).astype(BF16)

    x = _Rows(x_prompt.reshape(lp, d), x_sample.reshape(bs * ls, d))

    for layer in range(depth):
        i = layer // 2
        if layer % 2 == 0:
            h = _rmsnorm(x, mix_norm_even[i], out_dtype=BF16)
            u = _matmul(_Rows(h), w_in_even[i].astype(BF16), out_dtype=F32, tm=tm, tn=col_tile, name="even_in_proj")
            a = _pool_mixer(u, pool_w[i].astype(BF16), pool_scale[i], lay)
            va, vb = _chan_dft(u, pool_dim, fourier_dim // FOURIER_HEAD_DIM, lay)
            fr = _Rows(_seq_dft(va, vb, 0, bp, lp, MAX_ROW_TILE), _seq_dft(va, vb, lay.rows_p, bs, ls, MAX_ROW_TILE))
            b = _matmul(fr, fourier_w[i].astype(BF16), out_dtype=BF16, tm=tm, tn=col_tile, name="fourier_proj")
            x_new, xb, ss = _out_proj2(a, b, w_out_even[i].astype(BF16), x, tm=tm, tn=col_tile)
        else:
            assert layer > 0, "an odd layer follows a conv-FFN, which supplies bf16(x) and the row sums"
            w_qkv_b = (mix_norm_odd[i].astype(F32)[:, None] * w_qkv[i]).astype(BF16)
            qkv = _qkv_proj(xb, ss, w_qkv_b, lay, hd, hd ** -0.5 * LOG2_E, tn=min(QKV_COL_TILE, d))
            lams = (lambda_q1[i], lambda_k1[i], lambda_q2[i], lambda_k2[i])
            o = _Rows(_diff_attention(qkv, lams, subln_w[i], 0, bp, lp, d, hd, layer, ATTN_TILE_LONG),
                      _diff_attention(qkv, lams, subln_w[i], lay.rows_p, bs, ls, d, hd, layer, ATTN_TILE_SHORT))
            x_new, xb, ss = _matmul(o, w_out_odd[i].astype(BF16), resid=x, tm=tm, tn=col_tile, name="odd_out_proj")

        g = _ffn_up(xb, ss, w_up_b, layer, conv_w[layer], conv_b[layer], lay, tn=tn_ffn, f_pad=f_pad)
        x_new, xb, ss = _ffn_down(g, w_down_b, layer, x_new, tm=tm, tn=min(1024, d), tk=_down_k_tile(f_pad))
        x = _Rows(x_new)

    y_prompt = _rmsnorm(x, final_norm, out_dtype=F32, row0=0, rows=lay.rows_p)
    y_sample = _rmsnorm(x, final_norm, out_dtype=F32, row0=lay.rows_p, rows=lay.rows_s)
    return y_prompt.reshape(bp, lp, d), y_sample.reshape(bs, ls, d)
```

```python
import functools
import math

import numpy as np
import jax
import jax.numpy as jnp
from jax import lax
from jax.experimental import pallas as pl
from jax.experimental.pallas import tpu as pltpu

F32 = jnp.float32
BF16 = jnp.bfloat16

POOL_WINDOWS = (2, 4, 8, 16)
FOURIER_HEAD_DIM = 256
ROPE_THETA = 500000.0
NORM_EPS = 1e-6
SUBLN_EPS = 1e-5
LAMBDA_INIT_DECAY = 0.3

LANES = 128
SUBLANES_F32 = 8
SUBLANES_BF16 = 16
MXU_DIM = 256
VMEM_LIMIT_BYTES = 56 * 1024 * 1024

MAX_ROW_TILE = 1024
NORM_ROW_TILE = 256
FFN_COL_TILE = 512
QKV_COL_TILE = 1024
ATTN_TILE_LONG = (512, 1024)
ATTN_TILE_SHORT = (2048, 512)
DFT_ROW_SPLIT = 64
LOG2_E = math.log2(math.e)


def _params(*sem):
    return pltpu.CompilerParams(dimension_semantics=sem, vmem_limit_bytes=VMEM_LIMIT_BYTES)


class _Layout:
    def __init__(self, n_prompt, len_prompt, n_sample, len_sample):
        self.lp, self.ls = len_prompt, len_sample
        self.rows_p, self.rows_s = n_prompt * len_prompt, n_sample * len_sample
        self.rows = self.rows_p + self.rows_s
        self.tm = min(MAX_ROW_TILE, len_prompt, len_sample)
        assert len_prompt % self.tm == 0 and len_sample % self.tm == 0
        self.tiles_p = self.rows_p // self.tm
        self.tiles = self.rows // self.tm

    def seq_pos(self, i):
        in_p = i < self.tiles_p
        length = jnp.where(in_p, self.lp, self.ls)
        tile_in_seq = jnp.where(in_p, i % (self.lp // self.tm), (i - self.tiles_p) % (self.ls // self.tm))
        return length, tile_in_seq * self.tm


class _Rows:
    def __init__(self, *arrays):
        assert 1 <= len(arrays) <= 2
        self.arrays = arrays
        self.first_rows = arrays[0].shape[0]
        self.rows = sum(a.shape[0] for a in arrays)
        self.width = arrays[0].shape[1]

    def tiles_first(self, tm):
        assert self.first_rows % tm == 0
        return self.first_rows // tm

    def specs(self, tm, block_width, row=lambda *idx: idx[0], col=lambda *idx: 0):
        shape = (tm, block_width)
        if len(self.arrays) == 1:
            return [pl.BlockSpec(shape, lambda *idx: (row(*idx), col(*idx)))]
        tp = self.tiles_first(tm)
        return [pl.BlockSpec(shape, lambda *idx: (jnp.minimum(row(*idx), tp - 1), col(*idx))),
                pl.BlockSpec(shape, lambda *idx: (jnp.maximum(row(*idx) - tp, 0), col(*idx)))]


def _on_part(i, tiles_first, ref_groups, body):
    if len(ref_groups) == 1:
        body(*ref_groups[0])
        return
    pl.when(i < tiles_first)(lambda: body(*ref_groups[0]))
    pl.when(i >= tiles_first)(lambda: body(*ref_groups[1]))


def _lagged_tiles(n_i, n_j):
    n = n_i * n_j

    def cur(s):
        c = jnp.minimum(s, n - 1)
        return c // n_j, c % n_j

    def prev(s):
        p = jnp.maximum(s - 1, 0)
        return p // n_j, p % n_j

    return n, cur, prev


def _rmsnorm_kernel(*refs, eps, n_src, tiles_first):
    g_ref, o_ref = refs[n_src], refs[n_src + 1]

    def body(x_ref):
        x = x_ref[...]
        y = x * lax.rsqrt(jnp.mean(x * x, axis=-1, keepdims=True) + eps)
        o_ref[...] = (y * g_ref[...]).astype(o_ref.dtype)

    _on_part(pl.program_id(0), tiles_first, [(r,) for r in refs[:n_src]], body)


def _rmsnorm(src, g, *, out_dtype, row0=0, rows=None):
    d = src.width
    tm = NORM_ROW_TILE
    rows = src.rows - row0 if rows is None else rows
    rb0 = row0 // tm
    n_src = len(src.arrays)
    return pl.pallas_call(
        functools.partial(_rmsnorm_kernel, eps=NORM_EPS, n_src=n_src, tiles_first=src.tiles_first(tm)),
        grid=(rows // tm,),
        in_specs=src.specs(tm, d, row=lambda i: i + rb0) + [pl.BlockSpec((1, d), lambda i: (0, 0))],
        out_specs=pl.BlockSpec((tm, d), lambda i: (i, 0)),
        out_shape=jax.ShapeDtypeStruct((rows, d), out_dtype),
        compiler_params=_params("parallel"),
        name="rmsnorm",
    )(*src.arrays, g.reshape(1, d).astype(F32))


def _emit_stream(x, j, o_ref, xb_ref, ss_ref):
    if o_ref is not None:
        o_ref[...] = x
    xb_ref[...] = x.astype(xb_ref.dtype)
    part = jnp.broadcast_to(jnp.sum(x * x, axis=-1, keepdims=True), ss_ref.shape)

    @pl.when(j == 0)
    def _():
        ss_ref[...] = part

    @pl.when(j > 0)
    def _():
        ss_ref[...] += part


def _stream_outs(m, n, tm, tn):
    specs = [pl.BlockSpec((tm, tn), lambda i, j: (i, j)),
             pl.BlockSpec((tm, tn), lambda i, j: (i, j)),
             pl.BlockSpec((tm, LANES), lambda i, j: (i, 0))]
    shapes = [jax.ShapeDtypeStruct((m, n), F32), jax.ShapeDtypeStruct((m, n), BF16),
              jax.ShapeDtypeStruct((m, LANES), F32)]
    return specs, shapes


def _mm_kernel(*refs, n_a, n_r, tiles_first, stream):
    a_refs, b_ref = refs[:n_a], refs[n_a]
    r_refs = refs[n_a + 1:n_a + 1 + n_r]
    outs = refs[n_a + 1 + n_r:]
    i, j = pl.program_id(0), pl.program_id(1)

    def body(a_ref, r_ref):
        acc = jnp.dot(a_ref[...], b_ref[...], preferred_element_type=F32)
        if r_ref is not None:
            acc = acc + r_ref[...]
        if stream:
            _emit_stream(acc, j, *outs)
        else:
            outs[0][...] = acc.astype(outs[0].dtype)

    n_parts = max(n_a, n_r)
    groups = [(a_refs[min(p, n_a - 1)], r_refs[min(p, n_r - 1)] if n_r else None) for p in range(n_parts)]
    _on_part(i, tiles_first, groups, body)


def _matmul(a, b, *, tm, tn, name, out_dtype=None, resid=None):
    m, k = a.rows, a.width
    n = b.shape[1]
    srcs = [a] + ([resid] if resid is not None else [])
    firsts = {s.tiles_first(tm) for s in srcs if len(s.arrays) == 2}
    assert len(firsts) <= 1, "two-part operands must split at the same row"
    in_specs = a.specs(tm, k) + [pl.BlockSpec((k, tn), lambda i, j: (0, j))]
    args = list(a.arrays) + [b]
    if resid is not None:
        in_specs += resid.specs(tm, tn, col=lambda i, j: j)
        args += list(resid.arrays)
        out_specs, out_shape = _stream_outs(m, n, tm, tn)
    else:
        out_specs = pl.BlockSpec((tm, tn), lambda i, j: (i, j))
        out_shape = jax.ShapeDtypeStruct((m, n), out_dtype)
    return pl.pallas_call(
        functools.partial(_mm_kernel, n_a=len(a.arrays), n_r=len(resid.arrays) if resid is not None else 0,
                          tiles_first=firsts.pop() if firsts else 0, stream=resid is not None),
        grid=(m // tm, n // tn),
        in_specs=in_specs,
        out_specs=out_specs,
        out_shape=out_shape,
        compiler_params=_params("parallel", "arbitrary"),
        name=name,
    )(*args)


def _pool_kernel(up_ref, u_ref, un_ref, w_ref, s_ref, o_ref, *, lay, halo):
    g = pl.program_id(0)
    i = pl.program_id(1)
    tm = lay.tm
    length, pos0 = lay.seq_pos(i)
    x = u_ref[...]
    prev = jnp.where(pos0 == 0, 0.0, up_ref[...])
    nxt = jnp.where(pos0 + tm == length, 0.0, un_ref[...])
    e = jnp.concatenate([prev, x, nxt], axis=0)
    n = tm + 2 * halo
    s1 = e + pltpu.roll(e, 1, 0)
    t = pos0 + lax.broadcasted_iota(jnp.int32, (tm, 1), 0)

    for gi, w in enumerate(POOL_WINDOWS):
        h = w // 2

        @pl.when(g == gi)
        def _(h=h):
            s, k = s1, 1
            while k < h:
                s = pltpu.roll(s, k, 0) + pltpu.roll(s, n - k, 0)
                k *= 2
            win = s[halo:halo + tm]
            cnt = (jnp.minimum(t + h, length) - jnp.maximum(t - h, 0)).astype(F32)
            pooled = win / cnt - x
            y = jnp.dot(pooled.astype(BF16), w_ref[...], preferred_element_type=F32)
            o_ref[...] = (y * s_ref[...]).astype(o_ref.dtype)


def _pool_mixer(u, pool_w, pool_scale, lay):
    n_groups, c, _ = pool_w.shape
    tm, halo = lay.tm, SUBLANES_F32
    assert max(POOL_WINDOWS) // 2 <= halo and len(POOL_WINDOWS) == n_groups
    hb = tm // halo
    last_hb = lay.rows // halo - 1
    return pl.pallas_call(
        functools.partial(_pool_kernel, lay=lay, halo=halo),
        grid=(n_groups, lay.tiles),
        in_specs=[
            pl.BlockSpec((halo, c), lambda g, i: (jnp.maximum(i * hb - 1, 0), g)),
            pl.BlockSpec((tm, c), lambda g, i: (i, g)),
            pl.BlockSpec((halo, c), lambda g, i: (jnp.minimum((i + 1) * hb, last_hb), g)),
            pl.BlockSpec((None, c, c), lambda g, i: (g, 0, 0)),
            pl.BlockSpec((1, c), lambda g, i: (0, g)),
        ],
        out_specs=pl.BlockSpec((tm, c), lambda g, i: (i, g)),
        out_shape=jax.ShapeDtypeStruct((lay.rows, n_groups * c), BF16),
        compiler_params=_params("parallel", "parallel"),
        name="pool_mixer",
    )(u, u, u, pool_w, pool_scale.reshape(1, n_groups * c).astype(F32))


def _chan_dft_kernel(u_ref, w_ref, a_ref, b_ref, *, n):
    ab = jnp.dot(u_ref[...].astype(BF16), w_ref[...], preferred_element_type=F32)
    a_ref[...] = ab[:, :n].astype(a_ref.dtype)
    b_ref[...] = ab[:, n:].astype(b_ref.dtype)


def _chan_dft(u, col0, heads, lay):
    n = FOURIER_HEAD_DIM
    jk = np.outer(np.arange(n), np.arange(n)) % n
    ang = 2.0 * np.pi * jk / n
    w = jnp.asarray(np.concatenate([np.cos(ang), np.sin(ang)], axis=1) / math.sqrt(n), dtype=BF16)
    cb0 = col0 // n
    out = jax.ShapeDtypeStruct((lay.rows, heads * n), BF16)
    return pl.pallas_call(
        functools.partial(_chan_dft_kernel, n=n),
        grid=(heads, lay.tiles),
        in_specs=[pl.BlockSpec((lay.tm, n), lambda h, i: (i, cb0 + h)),
                  pl.BlockSpec((n, 2 * n), lambda h, i: (0, 0))],
        out_specs=[pl.BlockSpec((lay.tm, n), lambda h, i: (i, h))] * 2,
        out_shape=[out, out],
        compiler_params=_params("parallel", "parallel"),
        name="chan_dft",
    )(u, w)


def _trig_rows_kernel(c_ref, s_ref, *, length, mult, tr, tc):
    r = (lax.broadcasted_iota(jnp.int32, (tr, tc), 0) + pl.program_id(0) * tr) * mult
    c = lax.broadcasted_iota(jnp.int32, (tr, tc), 1) + pl.program_id(1) * tc
    ang = ((r * c) & (length - 1)).astype(F32) * (2.0 * math.pi / length)
    c_ref[...] = jnp.cos(ang)
    s_ref[...] = jnp.sin(ang)


def _trig_rows(length, n_rows, mult):
    tr, tc = SUBLANES_F32, min(length, 2048)
    out = jax.ShapeDtypeStruct((n_rows, length), F32)
    return pl.pallas_call(
        functools.partial(_trig_rows_kernel, length=length, mult=mult, tr=tr, tc=tc),
        grid=(n_rows // tr, length // tc),
        out_specs=[pl.BlockSpec((tr, tc), lambda i, j: (i, j))] * 2,
        out_shape=[out, out],
        compiler_params=_params("parallel", "parallel"),
        name="trig_rows",
    )()


def _dft_mat_kernel(ca_ref, sa_ref, cb_ref, sb_ref, c_ref, s_ref, *, split, groups):
    cb, sb = cb_ref[...], sb_ref[...]
    for q in range(groups):
        ca, sa = ca_ref[q:q + 1, :], sa_ref[q:q + 1, :]
        rows = slice(q * split, (q + 1) * split)
        c_ref[rows, :] = (ca * cb - sa * sb).astype(c_ref.dtype)
        s_ref[rows, :] = (-(sa * cb + ca * sb)).astype(s_ref.dtype)


def _dft_matrices(length):
    assert length & (length - 1) == 0
    split = min(DFT_ROW_SPLIT, length // SUBLANES_F32)
    groups = SUBLANES_F32
    ca, sa = _trig_rows(length, length // split, split)
    cb, sb = _trig_rows(length, split, 1)
    tc = min(length, 2048)
    out = jax.ShapeDtypeStruct((length, length), BF16)
    return pl.pallas_call(
        functools.partial(_dft_mat_kernel, split=split, groups=groups),
        grid=(length // (split * groups), length // tc),
        in_specs=[pl.BlockSpec((groups, tc), lambda i, j: (i, j))] * 2
        + [pl.BlockSpec((split, tc), lambda i, j: (0, j))] * 2,
        out_specs=[pl.BlockSpec((split * groups, tc), lambda i, j: (i, j))] * 2,
        out_shape=[out, out],
        compiler_params=_params("parallel", "parallel"),
        name="dft_matrices",
    )(ca, sa, cb, sb)


def _seq_dft_kernel(c_ref, s_ref, a_ref, b_ref, o_ref, acc_ref, *, scale):
    k = pl.program_id(2)

    @pl.when(k == 0)
    def _():
        acc_ref[...] = jnp.zeros_like(acc_ref)

    acc_ref[...] += (jnp.dot(c_ref[...], a_ref[...], preferred_element_type=F32)
                     + jnp.dot(s_ref[...], b_ref[...], preferred_element_type=F32))

    @pl.when(k == pl.num_programs(2) - 1)
    def _():
        o_ref[...] = (acc_ref[...] * scale).astype(o_ref.dtype)


def _seq_dft(va, vb, row0, n_seq, length, tile):
    cmat, nsmat = _dft_matrices(length)
    n = va.shape[1]
    tm = tk = min(tile, length)
    nt = length // tm
    rb0 = row0 // tk
    return pl.pallas_call(
        functools.partial(_seq_dft_kernel, scale=1.0 / math.sqrt(length)),
        grid=(n_seq, nt, nt),
        in_specs=[pl.BlockSpec((tm, tk), lambda s, i, k: (i, k)),
                  pl.BlockSpec((tm, tk), lambda s, i, k: (i, k)),
                  pl.BlockSpec((tk, n), lambda s, i, k: (rb0 + s * nt + k, 0)),
                  pl.BlockSpec((tk, n), lambda s, i, k: (rb0 + s * nt + k, 0))],
        out_specs=pl.BlockSpec((tm, n), lambda s, i, k: (s * nt + i, 0)),
        out_shape=jax.ShapeDtypeStruct((n_seq * length, n), BF16),
        scratch_shapes=[pltpu.VMEM((tm, n), F32)],
        compiler_params=_params("parallel", "parallel", "arbitrary"),
        name="seq_dft",
    )(cmat, nsmat, va, vb)


def _out_proj2_kernel(a_ref, b_ref, w1_ref, w2_ref, *rest, n_r, tiles_first):
    r_refs, outs = rest[:n_r], rest[n_r:]

    def body(r_ref):
        acc = jnp.dot(a_ref[...], w1_ref[...], preferred_element_type=F32)
        acc = acc + jnp.dot(b_ref[...], w2_ref[...], preferred_element_type=F32)
        _emit_stream(acc + r_ref[...], pl.program_id(1), *outs)

    _on_part(pl.program_id(0), tiles_first, [(r,) for r in r_refs], body)


def _out_proj2(a, b, w, resid, *, tm, tn):
    m, ka = a.shape
    kb = b.shape[1]
    n = w.shape[1]
    assert ka % kb == 0
    out_specs, out_shape = _stream_outs(m, n, tm, tn)
    return pl.pallas_call(
        functools.partial(_out_proj2_kernel, n_r=len(resid.arrays), tiles_first=resid.tiles_first(tm)),
        grid=(m // tm, n // tn),
        in_specs=[pl.BlockSpec((tm, ka), lambda i, j: (i, 0)),
                  pl.BlockSpec((tm, kb), lambda i, j: (i, 0)),
                  pl.BlockSpec((ka, tn), lambda i, j: (0, j)),
                  pl.BlockSpec((kb, tn), lambda i, j: (ka // kb, j))] + resid.specs(tm, tn, col=lambda i, j: j),
        out_specs=out_specs,
        out_shape=out_shape,
        compiler_params=_params("parallel", "arbitrary"),
        name="even_out_proj",
    )(a, b, w, w, *resid.arrays)


def _rope_tables_kernel(inv_ref, cos_ref, sa_ref, sb_ref, *, tr, half, length, extra, q_scale):
    row = lax.broadcasted_iota(jnp.int32, (tr, LANES), 0) + pl.program_id(0) * tr
    lane = lax.broadcasted_iota(jnp.int32, (tr, LANES), 1)
    scaled = row >= length + extra
    pos = jnp.where(scaled, row - (length + extra), row)
    ang = pos.astype(F32) * inv_ref[...]
    c, s = jnp.cos(ang), jnp.sin(ang)
    rot = (row < length) | scaled
    scale = jnp.where(scaled, q_scale, 1.0)
    cos_ref[...] = jnp.where(rot & (lane < 2 * half), c, 1.0) * scale
    sa_ref[...] = jnp.where(rot & (lane >= half) & (lane < 2 * half), s, 0.0) * scale
    sb_ref[...] = jnp.where(rot & (lane < half), -s, 0.0) * scale


def _rope_tables(length, head_dim, extra_rows, q_scale):
    assert head_dim == LANES
    rot = head_dim // 4
    half = rot // 2
    inv = np.ones((1, LANES), np.float64)
    freqs = ROPE_THETA ** (-np.arange(0, rot, 2, dtype=np.float64) / rot)
    inv[0, :half] = freqs
    inv[0, half:rot] = freqs
    tr = min(length, 512)
    assert extra_rows % tr == 0
    out = jax.ShapeDtypeStruct((2 * length + extra_rows, LANES), F32)
    return pl.pallas_call(
        functools.partial(_rope_tables_kernel, tr=tr, half=half, length=length, extra=extra_rows, q_scale=q_scale),
        grid=((2 * length + extra_rows) // tr,),
        in_specs=[pl.BlockSpec((1, LANES), lambda i: (0, 0))],
        out_specs=[pl.BlockSpec((tr, LANES), lambda i: (i, 0))] * 3,
        out_shape=[out, out, out],
        compiler_params=_params("parallel"),
        name="rope_tables",
    )(jnp.asarray(inv, dtype=F32))


def _qkv_kernel(h_ref, w_ref, ss_ref, cos_ref, sa_ref, sb_ref, o_ref, y_ref, *, half, dim):
    @pl.when(pl.program_id(0) == 0)
    def _():
        y_ref[...] = jnp.zeros_like(y_ref)

    r = lax.rsqrt(ss_ref[...] * (1.0 / dim) + NORM_EPS)
    cos, sa, sb = cos_ref[...], sa_ref[...], sb_ref[...]
    for c in range(y_ref.shape[1] // LANES):
        yc = y_ref[:, c * LANES:(c + 1) * LANES] * r
        out = yc * cos + pltpu.roll(yc, half, 1) * sa + pltpu.roll(yc, LANES - half, 1) * sb
        o_ref[:, c * LANES:(c + 1) * LANES] = out.astype(o_ref.dtype)
    y_ref[...] = jnp.dot(h_ref[...], w_ref[...], preferred_element_type=F32)


def _qkv_proj(h, ss, w, lay, head_dim, q_scale, *, tn):
    m, k = h.shape
    n = w.shape[1]
    tm = lay.tm
    max_len = max(lay.lp, lay.ls)
    tables = _rope_tables(max_len, head_dim, tm, q_scale)
    q_tiles = (n // 3) // tn
    steps, cur, prev = _lagged_tiles(m // tm, n // tn)

    def pos_block(s):
        i, j = prev(s)
        in_p = i < lay.tiles_p
        blk = jnp.where(in_p, i % (lay.lp // tm), (i - lay.tiles_p) % (lay.ls // tm))
        identity = max_len // tm
        return jnp.where(j < q_tiles, identity + 1 + blk, jnp.where(j < 2 * q_tiles, blk, identity)), 0

    tab = pl.BlockSpec((tm, LANES), pos_block)
    return pl.pallas_call(
        functools.partial(_qkv_kernel, half=head_dim // 8, dim=k),
        grid=(steps + 1,),
        in_specs=[pl.BlockSpec((tm, k), lambda s: (cur(s)[0], 0)),
                  pl.BlockSpec((k, tn), lambda s: (0, cur(s)[1])),
                  pl.BlockSpec((tm, LANES), lambda s: (prev(s)[0], 0)),
                  tab, tab, tab],
        out_specs=pl.BlockSpec((tm, tn), prev),
        out_shape=jax.ShapeDtypeStruct((m, n), BF16),
        scratch_shapes=[pltpu.VMEM((tm, tn), F32)],
        compiler_params=_params("arbitrary"),
        name="qkv_proj",
    )(h, w, ss, *tables)


def _attn_kernel(lq1_ref, lk1_ref, lq2_ref, lk2_ref, q_ref, k_ref, v_ref, w_ref, o_ref, *, hd, lam_init, tk):
    lam = (jnp.exp(jnp.sum(lq1_ref[...] * lk1_ref[...], keepdims=True))
           - jnp.exp(jnp.sum(lq2_ref[...] * lk2_ref[...], keepdims=True)) + lam_init)
    q = q_ref[...]
    qm = (q[:, :hd], q[:, hd:])
    dn = (((1,), (1,)), ((), ()))
    m, l, acc = [None, None], [None, None], [None, None]
    for c in range(k_ref.shape[0] // tk):
        kc = k_ref[c * tk:(c + 1) * tk, :]
        vc = v_ref[c * tk:(c + 1) * tk, :]
        for mp in range(2):
            s = lax.dot_general(qm[mp], kc[:, mp * hd:(mp + 1) * hd], dn, preferred_element_type=F32)
            mc = jnp.max(s, axis=-1, keepdims=True)
            m_new = mc if c == 0 else jnp.maximum(m[mp], mc)
            p = jnp.exp2(s - m_new)
            ls = jnp.sum(p, axis=-1, keepdims=True)
            pv = jnp.dot(p.astype(BF16), vc, preferred_element_type=F32)
            if c == 0:
                l[mp], acc[mp] = ls, pv
            else:
                alpha = jnp.exp2(m[mp] - m_new)
                l[mp] = alpha * l[mp] + ls
                acc[mp] = alpha * acc[mp] + pv
            m[mp] = m_new
    o = acc[0] * (1.0 / l[0]) - acc[1] * (lam / l[1])
    o = o * lax.rsqrt(jnp.mean(o * o, axis=-1, keepdims=True) + SUBLN_EPS)
    o_ref[...] = (o * w_ref[...] * (1.0 - lam_init)).astype(o_ref.dtype)


def _diff_attention(qkv, lams, subln_w, row0, n_seq, length, d_model, hd, layer, tile):
    tq, tk = (min(t, length) for t in tile)
    heads = d_model // (2 * hd)
    lam_init = 0.8 - 0.6 * math.exp(-LAMBDA_INIT_DECAY * layer)
    qt = length // tq
    qb0, sb0 = row0 // tq, row0 // length
    vec = pl.BlockSpec((1, hd), lambda b, h, i: (0, 0))
    return pl.pallas_call(
        functools.partial(_attn_kernel, hd=hd, lam_init=lam_init, tk=tk),
        grid=(n_seq, heads, qt),
        in_specs=[vec, vec, vec, vec,
                  pl.BlockSpec((tq, 2 * hd), lambda b, h, i: (qb0 + b * qt + i, h)),
                  pl.BlockSpec((length, 2 * hd), lambda b, h, i: (sb0 + b, heads + h)),
                  pl.BlockSpec((length, 2 * hd), lambda b, h, i: (sb0 + b, 2 * heads + h)),
                  pl.BlockSpec((1, 2 * hd), lambda b, h, i: (0, 0))],
        out_specs=pl.BlockSpec((tq, 2 * hd), lambda b, h, i: (b * qt + i, h)),
        out_shape=jax.ShapeDtypeStruct((n_seq * length, d_model), BF16),
        compiler_params=_params("parallel", "parallel", "arbitrary"),
        name="diff_attention",
    )(*[v.reshape(1, hd).astype(F32) for v in lams], qkv, qkv, qkv, subln_w.reshape(1, 2 * hd).astype(F32))


def _ffn_up_kernel(hp_ref, h_ref, hn_ref, sp_ref, s_ref, sn_ref, wg_ref, wu0_ref, wu1_ref, cw_ref, cb_ref, g_ref,
                   lhs_ref, r_ref, *, lay, halo, tn, f):
    i, j = pl.program_id(0), pl.program_id(1)
    tm = lay.tm

    @pl.when(j == 0)
    def _():
        length, pos0 = lay.seq_pos(i)
        zero = jnp.zeros(hp_ref.shape, hp_ref.dtype)
        lhs_ref[0:halo, :] = jnp.where(pos0 == 0, zero, hp_ref[...])
        lhs_ref[halo:halo + tm, :] = h_ref[...]
        lhs_ref[halo + tm:, :] = jnp.where(pos0 + tm == length, zero, hn_ref[...])
        inv_dim = 1.0 / h_ref.shape[1]
        r_ref[0:halo, :] = lax.rsqrt(sp_ref[...] * inv_dim + NORM_EPS)
        r_ref[halo:halo + tm, :] = lax.rsqrt(s_ref[...] * inv_dim + NORM_EPS)
        r_ref[halo + tm:, :] = lax.rsqrt(sn_ref[...] * inv_dim + NORM_EPS)

    n = tm + 2 * halo
    lhs = lhs_ref[...]
    cw, cb = cw_ref[...], cb_ref[...]
    r = jnp.concatenate([r_ref[...]] * (tn // LANES), axis=1)

    def conv(u, c0):
        k = slice(c0, c0 + tn)
        u = u * r
        return (pltpu.roll(u, 1, 0)[halo:halo + tm] * cw[0:1, k] + u[halo:halo + tm] * cw[1:2, k]
                + pltpu.roll(u, n - 1, 0)[halo:halo + tm] * cw[2:3, k] + cb[:, k])

    gate = conv(jnp.dot(lhs, wg_ref[...], preferred_element_type=F32), 0)
    up = conv(jnp.concatenate([jnp.dot(lhs, wu0_ref[...], preferred_element_type=F32),
                               jnp.dot(lhs, wu1_ref[...], preferred_element_type=F32)], axis=1), tn)
    g = gate / (1.0 + jnp.exp(-gate)) * up
    col = j * tn + lax.broadcasted_iota(jnp.int32, (1, tn), 1)
    g_ref[...] = jnp.where(col < f, g, 0.0).astype(g_ref.dtype)


def _gate_up_tiles(x, f, f_pad, tn):
    r = x.shape[0]
    x = jnp.pad(x.reshape(r, 2, f), ((0, 0), (0, 0), (0, f_pad - f)))
    return jnp.swapaxes(x.reshape(r, 2, f_pad // tn, tn), 1, 2).reshape(r, 2 * f_pad)


def _ffn_up(h, ss, w_up, layer, conv_w, conv_b, lay, *, tn, f_pad):
    m, k = h.shape
    f = w_up.shape[2] // 2
    half = tn // 2
    assert f % half == 0
    tm, halo = lay.tm, SUBLANES_BF16
    hb = tm // halo
    last_hb = m // halo - 1
    up0, last_half = f // half, 2 * f // half - 1

    def with_halo(width):
        return [pl.BlockSpec((halo, width), lambda i, j: (jnp.maximum(i * hb - 1, 0), 0)),
                pl.BlockSpec((tm, width), lambda i, j: (i, 0)),
                pl.BlockSpec((halo, width), lambda i, j: (jnp.minimum((i + 1) * hb, last_hb), 0))]

    return pl.pallas_call(
        functools.partial(_ffn_up_kernel, lay=lay, halo=halo, tn=tn, f=f),
        grid=(m // tm, f_pad // tn),
        in_specs=with_halo(k) + with_halo(LANES) + [
                  pl.BlockSpec((None, k, tn), lambda i, j: (layer, 0, j)),
                  pl.BlockSpec((None, k, half), lambda i, j: (layer, 0, up0 + 2 * j)),
                  pl.BlockSpec((None, k, half), lambda i, j: (layer, 0, jnp.minimum(up0 + 2 * j + 1, last_half))),
                  pl.BlockSpec((3, 2 * tn), lambda i, j: (0, j)),
                  pl.BlockSpec((1, 2 * tn), lambda i, j: (0, j))],
        out_specs=pl.BlockSpec((tm, tn), lambda i, j: (i, j)),
        out_shape=jax.ShapeDtypeStruct((m, f_pad), BF16),
        scratch_shapes=[pltpu.VMEM((tm + 2 * halo, k), BF16), pltpu.VMEM((tm + 2 * halo, LANES), F32)],
        compiler_params=_params("parallel", "arbitrary"),
        name="ffn_up",
    )(h, h, h, ss, ss, ss, w_up, w_up, w_up,
      _gate_up_tiles(conv_w.astype(F32), f, f_pad, tn), _gate_up_tiles(conv_b.astype(F32).reshape(1, 2 * f), f, f_pad, tn))


def _ffn_down_kernel(g_ref, w_ref, r_ref, o_ref, xb_ref, ss_ref, *, last_rows):
    kk = pl.program_id(2)
    last = pl.num_programs(2) - 1
    tk = w_ref.shape[0]

    @pl.when(kk == 0)
    def _():
        o_ref[...] = r_ref[...]

    def accumulate(rows):
        o_ref[...] += jnp.dot(g_ref[:, :rows], w_ref[:rows, :], preferred_element_type=F32)

    if last_rows == tk:
        accumulate(tk)
    else:
        pl.when(kk < last)(lambda: accumulate(tk))
        pl.when(kk == last)(lambda: accumulate(last_rows))

    @pl.when(kk == last)
    def _():
        _emit_stream(o_ref[...], pl.program_id(1), None, xb_ref, ss_ref)


def _ffn_down(g, w, layer, resid, *, tm, tn, tk):
    m, k = g.shape
    f, n = w.shape[1], w.shape[2]
    k_tiles = -(-f // tk)
    last_rows = f - (k_tiles - 1) * tk
    assert k_tiles * tk <= k and last_rows % LANES == 0
    return pl.pallas_call(
        functools.partial(_ffn_down_kernel, last_rows=last_rows),
        grid=(m // tm, n // tn, k_tiles),
        in_specs=[pl.BlockSpec((tm, tk), lambda i, j, kk: (i, kk)),
                  pl.BlockSpec((None, tk, tn), lambda i, j, kk: (layer, kk, j)),
                  pl.BlockSpec((tm, tn), lambda i, j, kk: (i, j))],
        out_specs=[pl.BlockSpec((tm, tn), lambda i, j, kk: (i, j)),
                   pl.BlockSpec((tm, tn), lambda i, j, kk: (i, j)),
                   pl.BlockSpec((tm, LANES), lambda i, j, kk: (i, 0))],
        out_shape=[jax.ShapeDtypeStruct((m, n), F32), jax.ShapeDtypeStruct((m, n), BF16),
                   jax.ShapeDtypeStruct((m, LANES), F32)],
        compiler_params=_params("parallel", "arbitrary", "arbitrary"),
        name="ffn_down",
    )(g, w, resid)


def _down_k_tile(f_pad):
    units = f_pad // MXU_DIM
    best = 1
    for d in range(1, units + 1):
        if units % d == 0 and d * MXU_DIM * 4 <= max(f_pad, 4 * MXU_DIM):
            best = d
    return best * MXU_DIM


def kernel(x_prompt, x_sample, mix_norm_even, w_in_even, pool_w, pool_scale, fourier_w, w_out_even, mix_norm_odd,
           w_qkv, lambda_q1, lambda_k1, lambda_q2, lambda_k2, subln_w, w_out_odd, ffn_norm, w_up, conv_w, conv_b,
           w_down, final_norm):
    bp, lp, d = x_prompt.shape
    bs, ls, _ = x_sample.shape
    assert bp == 1, "the prompt group is handled as one sequence"
    lay = _Layout(bp, lp, bs, ls)
    tm = lay.tm
    depth = ffn_norm.shape[0]
    f = w_down.shape[1]
    tn_ffn = FFN_COL_TILE
    f_pad = -(-f // tn_ffn) * tn_ffn
    hd = lambda_q1.shape[1]
    pool_dim = pool_w.shape[1] * pool_w.shape[2]
    fourier_dim = fourier_w.shape[1]
    col_tile = min(512, d)

    w_up_b = (ffn_norm.astype(F32)[:, :, None] * w_up).astype(BF16)
    w_down_b = w_down.astype(BF16)

    x = _Rows(x_prompt.reshape(lp, d), x_sample.reshape(bs * ls, d))

    for layer in range(depth):
        i = layer // 2
        if layer % 2 == 0:
            h = _rmsnorm(x, mix_norm_even[i], out_dtype=BF16)
            u = _matmul(_Rows(h), w_in_even[i].astype(BF16), out_dtype=F32, tm=tm, tn=col_tile, name="even_in_proj")
            a = _pool_mixer(u, pool_w[i].astype(BF16), pool_scale[i], lay)
            va, vb = _chan_dft(u, pool_dim, fourier_dim // FOURIER_HEAD_DIM, lay)
            fr = _Rows(_seq_dft(va, vb, 0, bp, lp, MAX_ROW_TILE), _seq_dft(va, vb, lay.rows_p, bs, ls, MAX_ROW_TILE))
            b = _matmul(fr, fourier_w[i].astype(BF16), out_dtype=BF16, tm=tm, tn=col_tile, name="fourier_proj")
            x_new, xb, ss = _out_proj2(a, b, w_out_even[i].astype(BF16), x, tm=tm, tn=col_tile)
        else:
            assert layer > 0, "an odd layer follows a conv-FFN, which supplies bf16(x) and the row sums"
            w_qkv_b = (mix_norm_odd[i].astype(F32)[:, None] * w_qkv[i]).astype(BF16)
            qkv = _qkv_proj(xb, ss, w_qkv_b, lay, hd, hd ** -0.5 * LOG2_E, tn=min(QKV_COL_TILE, d))
            lams = (lambda_q1[i], lambda_k1[i], lambda_q2[i], lambda_k2[i])
            o = _Rows(_diff_attention(qkv, lams, subln_w[i], 0, bp, lp, d, hd, layer, ATTN_TILE_LONG),
                      _diff_attention(qkv, lams, subln_w[i], lay.rows_p, bs, ls, d, hd, layer, ATTN_TILE_SHORT))
            x_new, xb, ss = _matmul(o, w_out_odd[i].astype(BF16), resid=x, tm=tm, tn=col_tile, name="odd_out_proj")

        g = _ffn_up(xb, ss, w_up_b, layer, conv_w[layer], conv_b[layer], lay, tn=tn_ffn, f_pad=f_pad)
        x_new, xb, ss = _ffn_down(g, w_down_b, layer, x_new, tm=tm, tn=min(1024, d), tk=_down_k_tile(f_pad))
        x = _Rows(x_new)

    y_prompt = _rmsnorm(x, final_norm, out_dtype=F32, row0=0, rows=lay.rows_p)
    y_sample = _rmsnorm(x, final_norm, out_dtype=F32, row0=lay.rows_p, rows=lay.rows_s)
    return y_prompt.reshape(bp, lp, d), y_sample.reshape(bs, ls, d)
```

```python
import functools
import math

import numpy as np
import jax
import jax.numpy as jnp
from jax import lax
from jax.experimental import pallas as pl
from jax.experimental.pallas import tpu as pltpu

F32 = jnp.float32
BF16 = jnp.bfloat16

POOL_WINDOWS = (2, 4, 8, 16)
FOURIER_HEAD_DIM = 256
ROPE_THETA = 500000.0
NORM_EPS = 1e-6
SUBLN_EPS = 1e-5
LAMBDA_INIT_DECAY = 0.3

LANES = 128
SUBLANES_F32 = 8
SUBLANES_BF16 = 16
MXU_DIM = 256
VMEM_LIMIT_BYTES = 56 * 1024 * 1024

MAX_ROW_TILE = 1024
NORM_ROW_TILE = 256
FFN_COL_TILE = 512
QKV_COL_TILE = 1024
ATTN_TILE_LONG = (512, 1024)
ATTN_TILE_SHORT = (2048, 512)
DFT_ROW_SPLIT = 64
LOG2_E = math.log2(math.e)


def _params(*sem):
    return pltpu.CompilerParams(dimension_semantics=sem, vmem_limit_bytes=VMEM_LIMIT_BYTES)


class _Layout:
    def __init__(self, n_prompt, len_prompt, n_sample, len_sample):
        self.lp, self.ls = len_prompt, len_sample
        self.rows_p, self.rows_s = n_prompt * len_prompt, n_sample * len_sample
        self.rows = self.rows_p + self.rows_s
        self.tm = min(MAX_ROW_TILE, len_prompt, len_sample)
        assert len_prompt % self.tm == 0 and len_sample % self.tm == 0
        self.tiles_p = self.rows_p // self.tm
        self.tiles = self.rows // self.tm

    def seq_pos(self, i):
        in_p = i < self.tiles_p
        length = jnp.where(in_p, self.lp, self.ls)
        tile_in_seq = jnp.where(in_p, i % (self.lp // self.tm), (i - self.tiles_p) % (self.ls // self.tm))
        return length, tile_in_seq * self.tm


class _Rows:
    def __init__(self, *arrays):
        assert 1 <= len(arrays) <= 2
        self.arrays = arrays
        self.first_rows = arrays[0].shape[0]
        self.rows = sum(a.shape[0] for a in arrays)
        self.width = arrays[0].shape[1]

    def tiles_first(self, tm):
        assert self.first_rows % tm == 0
        return self.first_rows // tm

    def specs(self, tm, block_width, row=lambda *idx: idx[0], col=lambda *idx: 0):
        shape = (tm, block_width)
        if len(self.arrays) == 1:
            return [pl.BlockSpec(shape, lambda *idx: (row(*idx), col(*idx)))]
        tp = self.tiles_first(tm)
        return [pl.BlockSpec(shape, lambda *idx: (jnp.minimum(row(*idx), tp - 1), col(*idx))),
                pl.BlockSpec(shape, lambda *idx: (jnp.maximum(row(*idx) - tp, 0), col(*idx)))]


def _on_part(i, tiles_first, ref_groups, body):
    if len(ref_groups) == 1:
        body(*ref_groups[0])
        return
    pl.when(i < tiles_first)(lambda: body(*ref_groups[0]))
    pl.when(i >= tiles_first)(lambda: body(*ref_groups[1]))


def _lagged_tiles(n_i, n_j):
    n = n_i * n_j

    def cur(s):
        c = jnp.minimum(s, n - 1)
        return c // n_j, c % n_j

    def prev(s):
        p = jnp.maximum(s - 1, 0)
        return p // n_j, p % n_j

    return n, cur, prev


def _rmsnorm_kernel(*refs, eps, n_src, tiles_first):
    g_ref, o_ref = refs[n_src], refs[n_src + 1]

    def body(x_ref):
        x = x_ref[...]
        y = x * lax.rsqrt(jnp.mean(x * x, axis=-1, keepdims=True) + eps)
        o_ref[...] = (y * g_ref[...]).astype(o_ref.dtype)

    _on_part(pl.program_id(0), tiles_first, [(r,) for r in refs[:n_src]], body)


def _rmsnorm(src, g, *, out_dtype, row0=0, rows=None):
    d = src.width
    tm = NORM_ROW_TILE
    rows = src.rows - row0 if rows is None else rows
    rb0 = row0 // tm
    n_src = len(src.arrays)
    return pl.pallas_call(
        functools.partial(_rmsnorm_kernel, eps=NORM_EPS, n_src=n_src, tiles_first=src.tiles_first(tm)),
        grid=(rows // tm,),
        in_specs=src.specs(tm, d, row=lambda i: i + rb0) + [pl.BlockSpec((1, d), lambda i: (0, 0))],
        out_specs=pl.BlockSpec((tm, d), lambda i: (i, 0)),
        out_shape=jax.ShapeDtypeStruct((rows, d), out_dtype),
        compiler_params=_params("parallel"),
        name="rmsnorm",
    )(*src.arrays, g.reshape(1, d).astype(F32))


def _emit_stream(x, j, o_ref, xb_ref, ss_ref):
    if o_ref is not None:
        o_ref[...] = x
    xb_ref[...] = x.astype(xb_ref.dtype)
    part = jnp.broadcast_to(jnp.sum(x * x, axis=-1, keepdims=True), ss_ref.shape)

    @pl.when(j == 0)
    def _():
        ss_ref[...] = part

    @pl.when(j > 0)
    def _():
        ss_ref[...] += part


def _stream_outs(m, n, tm, tn):
    specs = [pl.BlockSpec((tm, tn), lambda i, j: (i, j)),
             pl.BlockSpec((tm, tn), lambda i, j: (i, j)),
             pl.BlockSpec((tm, LANES), lambda i, j: (i, 0))]
    shapes = [jax.ShapeDtypeStruct((m, n), F32), jax.ShapeDtypeStruct((m, n), BF16),
              jax.ShapeDtypeStruct((m, LANES), F32)]
    return specs, shapes


def _mm_kernel(*refs, n_a, n_r, tiles_first, stream):
    a_refs, b_ref = refs[:n_a], refs[n_a]
    r_refs = refs[n_a + 1:n_a + 1 + n_r]
    outs = refs[n_a + 1 + n_r:]
    i, j = pl.program_id(0), pl.program_id(1)

    def body(a_ref, r_ref):
        acc = jnp.dot(a_ref[...], b_ref[...], preferred_element_type=F32)
        if r_ref is not None:
            acc = acc + r_ref[...]
        if stream:
            _emit_stream(acc, j, *outs)
        else:
            outs[0][...] = acc.astype(outs[0].dtype)

    n_parts = max(n_a, n_r)
    groups = [(a_refs[min(p, n_a - 1)], r_refs[min(p, n_r - 1)] if n_r else None) for p in range(n_parts)]
    _on_part(i, tiles_first, groups, body)


def _matmul(a, b, *, tm, tn, name, out_dtype=None, resid=None):
    m, k = a.rows, a.width
    n = b.shape[1]
    srcs = [a] + ([resid] if resid is not None else [])
    firsts = {s.tiles_first(tm) for s in srcs if len(s.arrays) == 2}
    assert len(firsts) <= 1, "two-part operands must split at the same row"
    in_specs = a.specs(tm, k) + [pl.BlockSpec((k, tn), lambda i, j: (0, j))]
    args = list(a.arrays) + [b]
    if resid is not None:
        in_specs += resid.specs(tm, tn, col=lambda i, j: j)
        args += list(resid.arrays)
        out_specs, out_shape = _stream_outs(m, n, tm, tn)
    else:
        out_specs = pl.BlockSpec((tm, tn), lambda i, j: (i, j))
        out_shape = jax.ShapeDtypeStruct((m, n), out_dtype)
    return pl.pallas_call(
        functools.partial(_mm_kernel, n_a=len(a.arrays), n_r=len(resid.arrays) if resid is not None else 0,
                          tiles_first=firsts.pop() if firsts else 0, stream=resid is not None),
        grid=(m // tm, n // tn),
        in_specs=in_specs,
        out_specs=out_specs,
        out_shape=out_shape,
        compiler_params=_params("parallel", "arbitrary"),
        name=name,
    )(*args)


def _pool_kernel(up_ref, u_ref, un_ref, w_ref, s_ref, o_ref, *, lay, halo):
    g = pl.program_id(0)
    i = pl.program_id(1)
    tm = lay.tm
    length, pos0 = lay.seq_pos(i)
    x = u_ref[...]
    prev = jnp.where(pos0 == 0, 0.0, up_ref[...])
    nxt = jnp.where(pos0 + tm == length, 0.0, un_ref[...])
    e = jnp.concatenate([prev, x, nxt], axis=0)
    n = tm + 2 * halo
    s1 = e + pltpu.roll(e, 1, 0)
    t = pos0 + lax.broadcasted_iota(jnp.int32, (tm, 1), 0)

    for gi, w in enumerate(POOL_WINDOWS):
        h = w // 2

        @pl.when(g == gi)
        def _(h=h):
            s, k = s1, 1
            while k < h:
                s = pltpu.roll(s, k, 0) + pltpu.roll(s, n - k, 0)
                k *= 2
            win = s[halo:halo + tm]
            cnt = (jnp.minimum(t + h, length) - jnp.maximum(t - h, 0)).astype(F32)
            pooled = win / cnt - x
            y = jnp.dot(pooled.astype(BF16), w_ref[...], preferred_element_type=F32)
            o_ref[...] = (y * s_ref[...]).astype(o_ref.dtype)


def _pool_mixer(u, pool_w, pool_scale, lay):
    n_groups, c, _ = pool_w.shape
    tm, halo = lay.tm, SUBLANES_F32
    assert max(POOL_WINDOWS) // 2 <= halo and len(POOL_WINDOWS) == n_groups
    hb = tm // halo
    last_hb = lay.rows // halo - 1
    return pl.pallas_call(
        functools.partial(_pool_kernel, lay=lay, halo=halo),
        grid=(n_groups, lay.tiles),
        in_specs=[
            pl.BlockSpec((halo, c), lambda g, i: (jnp.maximum(i * hb - 1, 0), g)),
            pl.BlockSpec((tm, c), lambda g, i: (i, g)),
            pl.BlockSpec((halo, c), lambda g, i: (jnp.minimum((i + 1) * hb, last_hb), g)),
            pl.BlockSpec((None, c, c), lambda g, i: (g, 0, 0)),
            pl.BlockSpec((1, c), lambda g, i: (0, g)),
        ],
        out_specs=pl.BlockSpec((tm, c), lambda g, i: (i, g)),
        out_shape=jax.ShapeDtypeStruct((lay.rows, n_groups * c), BF16),
        compiler_params=_params("parallel", "parallel"),
        name="pool_mixer",
    )(u, u, u, pool_w, pool_scale.reshape(1, n_groups * c).astype(F32))


def _chan_dft_kernel(u_ref, w_ref, a_ref, b_ref, *, n):
    ab = jnp.dot(u_ref[...].astype(BF16), w_ref[...], preferred_element_type=F32)
    a_ref[...] = ab[:, :n].astype(a_ref.dtype)
    b_ref[...] = ab[:, n:].astype(b_ref.dtype)


def _chan_dft(u, col0, heads, lay):
    n = FOURIER_HEAD_DIM
    jk = np.outer(np.arange(n), np.arange(n)) % n
    ang = 2.0 * np.pi * jk / n
    w = jnp.asarray(np.concatenate([np.cos(ang), np.sin(ang)], axis=1) / math.sqrt(n), dtype=BF16)
    cb0 = col0 // n
    out = jax.ShapeDtypeStruct((lay.rows, heads * n), BF16)
    return pl.pallas_call(
        functools.partial(_chan_dft_kernel, n=n),
        grid=(heads, lay.tiles),
        in_specs=[pl.BlockSpec((lay.tm, n), lambda h, i: (i, cb0 + h)),
                  pl.BlockSpec((n, 2 * n), lambda h, i: (0, 0))],
        out_specs=[pl.BlockSpec((lay.tm, n), lambda h, i: (i, h))] * 2,
        out_shape=[out, out],
        compiler_params=_params("parallel", "parallel"),
        name="chan_dft",
    )(u, w)


def _trig_rows_kernel(c_ref, s_ref, *, length, mult, tr, tc):
    r = (lax.broadcasted_iota(jnp.int32, (tr, tc), 0) + pl.program_id(0) * tr) * mult
    c = lax.broadcasted_iota(jnp.int32, (tr, tc), 1) + pl.program_id(1) * tc
    ang = ((r * c) & (length - 1)).astype(F32) * (2.0 * math.pi / length)
    c_ref[...] = jnp.cos(ang)
    s_ref[...] = jnp.sin(ang)


def _trig_rows(length, n_rows, mult):
    tr, tc = SUBLANES_F32, min(length, 2048)
    out = jax.ShapeDtypeStruct((n_rows, length), F32)
    return pl.pallas_call(
        functools.partial(_trig_rows_kernel, length=length, mult=mult, tr=tr, tc=tc),
        grid=(n_rows // tr, length // tc),
        out_specs=[pl.BlockSpec((tr, tc), lambda i, j: (i, j))] * 2,
        out_shape=[out, out],
        compiler_params=_params("parallel", "parallel"),
        name="trig_rows",
    )()


def _dft_mat_kernel(ca_ref, sa_ref, cb_ref, sb_ref, c_ref, s_ref, *, split, groups):
    cb, sb = cb_ref[...], sb_ref[...]
    for q in range(groups):
        ca, sa = ca_ref[q:q + 1, :], sa_ref[q:q + 1, :]
        rows = slice(q * split, (q + 1) * split)
        c_ref[rows, :] = (ca * cb - sa * sb).astype(c_ref.dtype)
        s_ref[rows, :] = (-(sa * cb + ca * sb)).astype(s_ref.dtype)


def _dft_matrices(length):
    assert length & (length - 1) == 0
    split = min(DFT_ROW_SPLIT, length // SUBLANES_F32)
    groups = SUBLANES_F32
    ca, sa = _trig_rows(length, length // split, split)
    cb, sb = _trig_rows(length, split, 1)
    tc = min(length, 2048)
    out = jax.ShapeDtypeStruct((length, length), BF16)
    return pl.pallas_call(
        functools.partial(_dft_mat_kernel, split=split, groups=groups),
        grid=(length // (split * groups), length // tc),
        in_specs=[pl.BlockSpec((groups, tc), lambda i, j: (i, j))] * 2
        + [pl.BlockSpec((split, tc), lambda i, j: (0, j))] * 2,
        out_specs=[pl.BlockSpec((split * groups, tc), lambda i, j: (i, j))] * 2,
        out_shape=[out, out],
        compiler_params=_params("parallel", "parallel"),
        name="dft_matrices",
    )(ca, sa, cb, sb)


def _seq_dft_kernel(c_ref, s_ref, a_ref, b_ref, o_ref, acc_ref, *, scale):
    k = pl.program_id(2)

    @pl.when(k == 0)
    def _():
        acc_ref[...] = jnp.zeros_like(acc_ref)

    acc_ref[...] += (jnp.dot(c_ref[...], a_ref[...], preferred_element_type=F32)
                     + jnp.dot(s_ref[...], b_ref[...], preferred_element_type=F32))

    @pl.when(k == pl.num_programs(2) - 1)
    def _():
        o_ref[...] = (acc_ref[...] * scale).astype(o_ref.dtype)


def _seq_dft(va, vb, row0, n_seq, length, tile):
    cmat, nsmat = _dft_matrices(length)
    n = va.shape[1]
    tm = tk = min(tile, length)
    nt = length // tm
    rb0 = row0 // tk
    return pl.pallas_call(
        functools.partial(_seq_dft_kernel, scale=1.0 / math.sqrt(length)),
        grid=(n_seq, nt, nt),
        in_specs=[pl.BlockSpec((tm, tk), lambda s, i, k: (i, k)),
                  pl.BlockSpec((tm, tk), lambda s, i, k: (i, k)),
                  pl.BlockSpec((tk, n), lambda s, i, k: (rb0 + s * nt + k, 0)),
                  pl.BlockSpec((tk, n), lambda s, i, k: (rb0 + s * nt + k, 0))],
        out_specs=pl.BlockSpec((tm, n), lambda s, i, k: (s * nt + i, 0)),
        out_shape=jax.ShapeDtypeStruct((n_seq * length, n), BF16),
        scratch_shapes=[pltpu.VMEM((tm, n), F32)],
        compiler_params=_params("parallel", "parallel", "arbitrary"),
        name="seq_dft",
    )(cmat, nsmat, va, vb)


def _out_proj2_kernel(a_ref, b_ref, w1_ref, w2_ref, *rest, n_r, tiles_first):
    r_refs, outs = rest[:n_r], rest[n_r:]

    def body(r_ref):
        acc = jnp.dot(a_ref[...], w1_ref[...], preferred_element_type=F32)
        acc = acc + jnp.dot(b_ref[...], w2_ref[...], preferred_element_type=F32)
        _emit_stream(acc + r_ref[...], pl.program_id(1), *outs)

    _on_part(pl.program_id(0), tiles_first, [(r,) for r in r_refs], body)


def _out_proj2(a, b, w, resid, *, tm, tn):
    m, ka = a.shape
    kb = b.shape[1]
    n = w.shape[1]
    assert ka % kb == 0
    out_specs, out_shape = _stream_outs(m, n, tm, tn)
    return pl.pallas_call(
        functools.partial(_out_proj2_kernel, n_r=len(resid.arrays), tiles_first=resid.tiles_first(tm)),
        grid=(m // tm, n // tn),
        in_specs=[pl.BlockSpec((tm, ka), lambda i, j: (i, 0)),
                  pl.BlockSpec((tm, kb), lambda i, j: (i, 0)),
                  pl.BlockSpec((ka, tn), lambda i, j: (0, j)),
                  pl.BlockSpec((kb, tn), lambda i, j: (ka // kb, j))] + resid.specs(tm, tn, col=lambda i, j: j),
        out_specs=out_specs,
        out_shape=out_shape,
        compiler_params=_params("parallel", "arbitrary"),
        name="even_out_proj",
    )(a, b, w, w, *resid.arrays)


def _rope_tables_kernel(inv_ref, cos_ref, sa_ref, sb_ref, *, tr, half, length, extra, q_scale):
    row = lax.broadcasted_iota(jnp.int32, (tr, LANES), 0) + pl.program_id(0) * tr
    lane = lax.broadcasted_iota(jnp.int32, (tr, LANES), 1)
    scaled = row >= length + extra
    pos = jnp.where(scaled, row - (length + extra), row)
    ang = pos.astype(F32) * inv_ref[...]
    c, s = jnp.cos(ang), jnp.sin(ang)
    rot = (row < length) | scaled
    scale = jnp.where(scaled, q_scale, 1.0)
    cos_ref[...] = jnp.where(rot & (lane < 2 * half), c, 1.0) * scale
    sa_ref[...] = jnp.where(rot & (lane >= half) & (lane < 2 * half), s, 0.0) * scale
    sb_ref[...] = jnp.where(rot & (lane < half), -s, 0.0) * scale


def _rope_tables(length, head_dim, extra_rows, q_scale):
    assert head_dim == LANES
    rot = head_dim // 4
    half = rot // 2
    inv = np.ones((1, LANES), np.float64)
    freqs = ROPE_THETA ** (-np.arange(0, rot, 2, dtype=np.float64) / rot)
    inv[0, :half] = freqs
    inv[0, half:rot] = freqs
    tr = min(length, 512)
    assert extra_rows % tr == 0
    out = jax.ShapeDtypeStruct((2 * length + extra_rows, LANES), F32)
    return pl.pallas_call(
        functools.partial(_rope_tables_kernel, tr=tr, half=half, length=length, extra=extra_rows, q_scale=q_scale),
        grid=((2 * length + extra_rows) // tr,),
        in_specs=[pl.BlockSpec((1, LANES), lambda i: (0, 0))],
        out_specs=[pl.BlockSpec((tr, LANES), lambda i: (i, 0))] * 3,
        out_shape=[out, out, out],
        compiler_params=_params("parallel"),
        name="rope_tables",
    )(jnp.asarray(inv, dtype=F32))


def _qkv_kernel(h_ref, w_ref, ss_ref, cos_ref, sa_ref, sb_ref, o_ref, y_ref, *, half, dim):
    @pl.when(pl.program_id(0) == 0)
    def _():
        y_ref[...] = jnp.zeros_like(y_ref)

    r = lax.rsqrt(ss_ref[...] * (1.0 / dim) + NORM_EPS)
    cos, sa, sb = cos_ref[...], sa_ref[...], sb_ref[...]
    for c in range(y_ref.shape[1] // LANES):
        yc = y_ref[:, c * LANES:(c + 1) * LANES] * r
        out = yc * cos + pltpu.roll(yc, half, 1) * sa + pltpu.roll(yc, LANES - half, 1) * sb
        o_ref[:, c * LANES:(c + 1) * LANES] = out.astype(o_ref.dtype)
    y_ref[...] = jnp.dot(h_ref[...], w_ref[...], preferred_element_type=F32)


def _qkv_proj(h, ss, w, lay, head_dim, q_scale, *, tn):
    m, k = h.shape
    n = w.shape[1]
    tm = lay.tm
    max_len = max(lay.lp, lay.ls)
    tables = _rope_tables(max_len, head_dim, tm, q_scale)
    q_tiles = (n // 3) // tn
    steps, cur, prev = _lagged_tiles(m // tm, n // tn)

    def pos_block(s):
        i, j = prev(s)
        in_p = i < lay.tiles_p
        blk = jnp.where(in_p, i % (lay.lp // tm), (i - lay.tiles_p) % (lay.ls // tm))
        identity = max_len // tm
        return jnp.where(j < q_tiles, identity + 1 + blk, jnp.where(j < 2 * q_tiles, blk, identity)), 0

    tab = pl.BlockSpec((tm, LANES), pos_block)
    return pl.pallas_call(
        functools.partial(_qkv_kernel, half=head_dim // 8, dim=k),
        grid=(steps + 1,),
        in_specs=[pl.BlockSpec((tm, k), lambda s: (cur(s)[0], 0)),
                  pl.BlockSpec((k, tn), lambda s: (0, cur(s)[1])),
                  pl.BlockSpec((tm, LANES), lambda s: (prev(s)[0], 0)),
                  tab, tab, tab],
        out_specs=pl.BlockSpec((tm, tn), prev),
        out_shape=jax.ShapeDtypeStruct((m, n), BF16),
        scratch_shapes=[pltpu.VMEM((tm, tn), F32)],
        compiler_params=_params("arbitrary"),
        name="qkv_proj",
    )(h, w, ss, *tables)


def _attn_kernel(lq1_ref, lk1_ref, lq2_ref, lk2_ref, q_ref, k_ref, v_ref, w_ref, o_ref, *, hd, lam_init, tk):
    lam = (jnp.exp(jnp.sum(lq1_ref[...] * lk1_ref[...], keepdims=True))
           - jnp.exp(jnp.sum(lq2_ref[...] * lk2_ref[...], keepdims=True)) + lam_init)
    q = q_ref[...]
    qm = (q[:, :hd], q[:, hd:])
    dn = (((1,), (1,)), ((), ()))
    m, l, acc = [None, None], [None, None], [None, None]
    for c in range(k_ref.shape[0] // tk):
        kc = k_ref[c * tk:(c + 1) * tk, :]
        vc = v_ref[c * tk:(c + 1) * tk, :]
        for mp in range(2):
            s = lax.dot_general(qm[mp], kc[:, mp * hd:(mp + 1) * hd], dn, preferred_element_type=F32)
            mc = jnp.max(s, axis=-1, keepdims=True)
            m_new = mc if c == 0 else jnp.maximum(m[mp], mc)
            p = jnp.exp2(s - m_new)
            ls = jnp.sum(p, axis=-1, keepdims=True)
            pv = jnp.dot(p.astype(BF16), vc, preferred_element_type=F32)
            if c == 0:
                l[mp], acc[mp] = ls, pv
            else:
                alpha = jnp.exp2(m[mp] - m_new)
                l[mp] = alpha * l[mp] + ls
                acc[mp] = alpha * acc[mp] + pv
            m[mp] = m_new
    o = acc[0] * (1.0 / l[0]) - acc[1] * (lam / l[1])
    o = o * lax.rsqrt(jnp.mean(o * o, axis=-1, keepdims=True) + SUBLN_EPS)
    o_ref[...] = (o * w_ref[...] * (1.0 - lam_init)).astype(o_ref.dtype)


def _diff_attention(qkv, lams, subln_w, row0, n_seq, length, d_model, hd, layer, tile):
    tq, tk = (min(t, length) for t in tile)
    heads = d_model // (2 * hd)
    lam_init = 0.8 - 0.6 * math.exp(-LAMBDA_INIT_DECAY * layer)
    qt = length // tq
    qb0, sb0 = row0 // tq, row0 // length
    vec = pl.BlockSpec((1, hd), lambda b, h, i: (0, 0))
    return pl.pallas_call(
        functools.partial(_attn_kernel, hd=hd, lam_init=lam_init, tk=tk),
        grid=(n_seq, heads, qt),
        in_specs=[vec, vec, vec, vec,
                  pl.BlockSpec((tq, 2 * hd), lambda b, h, i: (qb0 + b * qt + i, h)),
                  pl.BlockSpec((length, 2 * hd), lambda b, h, i: (sb0 + b, heads + h)),
                  pl.BlockSpec((length, 2 * hd), lambda b, h, i: (sb0 + b, 2 * heads + h)),
                  pl.BlockSpec((1, 2 * hd), lambda b, h, i: (0, 0))],
        out_specs=pl.BlockSpec((tq, 2 * hd), lambda b, h, i: (b * qt + i, h)),
        out_shape=jax.ShapeDtypeStruct((n_seq * length, d_model), BF16),
        compiler_params=_params("parallel", "parallel", "arbitrary"),
        name="diff_attention",
    )(*[v.reshape(1, hd).astype(F32) for v in lams], qkv, qkv, qkv, subln_w.reshape(1, 2 * hd).astype(F32))


def _ffn_up_kernel(hp_ref, h_ref, hn_ref, sp_ref, s_ref, sn_ref, wg_ref, wu0_ref, wu1_ref, cw_ref, cb_ref, g_ref,
                   lhs_ref, r_ref, *, lay, halo, tn, f):
    i, j = pl.program_id(0), pl.program_id(1)
    tm = lay.tm

    @pl.when(j == 0)
    def _():
        length, pos0 = lay.seq_pos(i)
        zero = jnp.zeros(hp_ref.shape, hp_ref.dtype)
        lhs_ref[0:halo, :] = jnp.where(pos0 == 0, zero, hp_ref[...])
        lhs_ref[halo:halo + tm, :] = h_ref[...]
        lhs_ref[halo + tm:, :] = jnp.where(pos0 + tm == length, zero, hn_ref[...])
        inv_dim = 1.0 / h_ref.shape[1]
        r_ref[0:halo, :] = lax.rsqrt(sp_ref[...] * inv_dim + NORM_EPS)
        r_ref[halo:halo + tm, :] = lax.rsqrt(s_ref[...] * inv_dim + NORM_EPS)
        r_ref[halo + tm:, :] = lax.rsqrt(sn_ref[...] * inv_dim + NORM_EPS)

    n = tm + 2 * halo
    lhs = lhs_ref[...]
    cw, cb = cw_ref[...], cb_ref[...]
    r = jnp.concatenate([r_ref[...]] * (tn // LANES), axis=1)

    def conv(u, c0):
        k = slice(c0, c0 + tn)
        u = u * r
        return (pltpu.roll(u, 1, 0)[halo:halo + tm] * cw[0:1, k] + u[halo:halo + tm] * cw[1:2, k]
                + pltpu.roll(u, n - 1, 0)[halo:halo + tm] * cw[2:3, k] + cb[:, k])

    gate = conv(jnp.dot(lhs, wg_ref[...], preferred_element_type=F32), 0)
    up = conv(jnp.concatenate([jnp.dot(lhs, wu0_ref[...], preferred_element_type=F32),
                               jnp.dot(lhs, wu1_ref[...], preferred_element_type=F32)], axis=1), tn)
    g = gate / (1.0 + jnp.exp(-gate)) * up
    col = j * tn + lax.broadcasted_iota(jnp.int32, (1, tn), 1)
    g_ref[...] = jnp.where(col < f, g, 0.0).astype(g_ref.dtype)


def _gate_up_tiles(x, f, f_pad, tn):
    r = x.shape[0]
    x = jnp.pad(x.reshape(r, 2, f), ((0, 0), (0, 0), (0, f_pad - f)))
    return jnp.swapaxes(x.reshape(r, 2, f_pad // tn, tn), 1, 2).reshape(r, 2 * f_pad)


def _ffn_up(h, ss, w_up, layer, conv_w, conv_b, lay, *, tn, f_pad):
    m, k = h.shape
    f = w_up.shape[2] // 2
    half = tn // 2
    assert f % half == 0
    tm, halo = lay.tm, SUBLANES_BF16
    hb = tm // halo
    last_hb = m // halo - 1
    up0, last_half = f // half, 2 * f // half - 1

    def with_halo(width):
        return [pl.BlockSpec((halo, width), lambda i, j: (jnp.maximum(i * hb - 1, 0), 0)),
                pl.BlockSpec((tm, width), lambda i, j: (i, 0)),
                pl.BlockSpec((halo, width), lambda i, j: (jnp.minimum((i + 1) * hb, last_hb), 0))]

    return pl.pallas_call(
        functools.partial(_ffn_up_kernel, lay=lay, halo=halo, tn=tn, f=f),
        grid=(m // tm, f_pad // tn),
        in_specs=with_halo(k) + with_halo(LANES) + [
                  pl.BlockSpec((None, k, tn), lambda i, j: (layer, 0, j)),
                  pl.BlockSpec((None, k, half), lambda i, j: (layer, 0, up0 + 2 * j)),
                  pl.BlockSpec((None, k, half), lambda i, j: (layer, 0, jnp.minimum(up0 + 2 * j + 1, last_half))),
                  pl.BlockSpec((3, 2 * tn), lambda i, j: (0, j)),
                  pl.BlockSpec((1, 2 * tn), lambda i, j: (0, j))],
        out_specs=pl.BlockSpec((tm, tn), lambda i, j: (i, j)),
        out_shape=jax.ShapeDtypeStruct((m, f_pad), BF16),
        scratch_shapes=[pltpu.VMEM((tm + 2 * halo, k), BF16), pltpu.VMEM((tm + 2 * halo, LANES), F32)],
        compiler_params=_params("parallel", "arbitrary"),
        name="ffn_up",
    )(h, h, h, ss, ss, ss, w_up, w_up, w_up,
      _gate_up_tiles(conv_w.astype(F32), f, f_pad, tn), _gate_up_tiles(conv_b.astype(F32).reshape(1, 2 * f), f, f_pad, tn))


def _ffn_down_kernel(g_ref, w_ref, r_ref, o_ref, xb_ref, ss_ref, *, last_rows):
    kk = pl.program_id(2)
    last = pl.num_programs(2) - 1
    tk = w_ref.shape[0]

    @pl.when(kk == 0)
    def _():
        o_ref[...] = r_ref[...]

    def accumulate(rows):
        o_ref[...] += jnp.dot(g_ref[:, :rows], w_ref[:rows, :], preferred_element_type=F32)

    if last_rows == tk:
        accumulate(tk)
    else:
        pl.when(kk < last)(lambda: accumulate(tk))
        pl.when(kk == last)(lambda: accumulate(last_rows))

    @pl.when(kk == last)
    def _():
        _emit_stream(o_ref[...], pl.program_id(1), None, xb_ref, ss_ref)


def _ffn_down(g, w, layer, resid, *, tm, tn, tk):
    m, k = g.shape
    f, n = w.shape[1], w.shape[2]
    k_tiles = -(-f // tk)
    last_rows = f - (k_tiles - 1) * tk
    assert k_tiles * tk <= k and last_rows % LANES == 0
    return pl.pallas_call(
        functools.partial(_ffn_down_kernel, last_rows=last_rows),
        grid=(m // tm, n // tn, k_tiles),
        in_specs=[pl.BlockSpec((tm, tk), lambda i, j, kk: (i, kk)),
                  pl.BlockSpec((None, tk, tn), lambda i, j, kk: (layer, kk, j)),
                  pl.BlockSpec((tm, tn), lambda i, j, kk: (i, j))],
        out_specs=[pl.BlockSpec((tm, tn), lambda i, j, kk: (i, j)),
                   pl.BlockSpec((tm, tn), lambda i, j, kk: (i, j)),
                   pl.BlockSpec((tm, LANES), lambda i, j, kk: (i, 0))],
        out_shape=[jax.ShapeDtypeStruct((m, n), F32), jax.ShapeDtypeStruct((m, n), BF16),
                   jax.ShapeDtypeStruct((m, LANES), F32)],
        compiler_params=_params("parallel", "arbitrary", "arbitrary"),
        name="ffn_down",
    )(g, w, resid)


def _down_k_tile(f_pad):
    units = f_pad // MXU_DIM
    best = 1
    for d in range(1, units + 1):
        if units % d == 0 and d * MXU_DIM * 4 <= max(f_pad, 4 * MXU_DIM):
            best = d
    return best * MXU_DIM


def kernel(x_prompt, x_sample, mix_norm_even, w_in_even, pool_w, pool_scale, fourier_w, w_out_even, mix_norm_odd,
           w_qkv, lambda_q1, lambda_k1, lambda_q2, lambda_k2, subln_w, w_out_odd, ffn_norm, w_up, conv_w, conv_b,
           w_down, final_norm):
    bp, lp, d = x_prompt.shape
    bs, ls, _ = x_sample.shape
    assert bp == 1, "the prompt group is handled as one sequence"
    lay = _Layout(bp, lp, bs, ls)
    tm = lay.tm
    depth = ffn_norm.shape[0]
    f = w_down.shape[1]
    tn_ffn = FFN_COL_TILE
    f_pad = -(-f // tn_ffn) * tn_ffn
    hd = lambda_q1.shape[1]
    pool_dim = pool_w.shape[1] * pool_w.shape[2]
    fourier_dim = fourier_w.shape[1]
    col_tile = min(512, d)

    w_up_b = (ffn_norm.astype(F32)[:, :, None] * w_up).astype(BF16)
    w_down_b = w_down.astype(BF16)

    x = _Rows(x_prompt.reshape(lp, d), x_sample.reshape(bs * ls, d))

    for layer in range(depth):
        i = layer // 2
        if layer % 2 == 0:
            h = _rmsnorm(x, mix_norm_even[i], out_dtype=BF16)
            u = _matmul(_Rows(h), w_in_even[i].astype(BF16), out_dtype=F32, tm=tm, tn=col_tile, name="even_in_proj")
            a = _pool_mixer(u, pool_w[i].astype(BF16), pool_scale[i], lay)
            va, vb = _chan_dft(u, pool_dim, fourier_dim // FOURIER_HEAD_DIM, lay)
            fr = _Rows(_seq_dft(va, vb, 0, bp, lp, MAX_ROW_TILE), _seq_dft(va, vb, lay.rows_p, bs, ls, MAX_ROW_TILE))
            b = _matmul(fr, fourier_w[i].astype(BF16), out_dtype=BF16, tm=tm, tn=col_tile, name="fourier_proj")
            x_new, xb, ss = _out_proj2(a, b, w_out_even[i].astype(BF16), x, tm=tm, tn=col_tile)
        else:
            assert layer > 0, "an odd layer follows a conv-FFN, which supplies bf16(x) and the row sums"
            w_qkv_b = (mix_norm_odd[i].astype(F32)[:, None] * w_qkv[i]).astype(BF16)
            qkv = _qkv_proj(xb, ss, w_qkv_b, lay, hd, hd ** -0.5 * LOG2_E, tn=min(QKV_COL_TILE, d))
            lams = (lambda_q1[i], lambda_k1[i], lambda_q2[i], lambda_k2[i])
            o = _Rows(_diff_attention(qkv, lams, subln_w[i], 0, bp, lp, d, hd, layer, ATTN_TILE_LONG),
                      _diff_attention(qkv, lams, subln_w[i], lay.rows_p, bs, ls, d, hd, layer, ATTN_TILE_SHORT))
            x_new, xb, ss = _matmul(o, w_out_odd[i].astype(BF16), resid=x, tm=tm, tn=col_tile, name="odd_out_proj")

        g = _ffn_up(xb, ss, w_up_b, layer, conv_w[layer], conv_b[layer], lay, tn=tn_ffn, f_pad=f_pad)
        x_new, xb, ss = _ffn_down(g, w_down_b, layer, x_new, tm=tm, tn=min(512, d), tk=_down_k_tile(2 * f_pad))
        x = _Rows(x_new)

    y_prompt = _rmsnorm(x, final_norm, out_dtype=F32, row0=0, rows=lay.rows_p)
    y_sample = _rmsnorm(x, final_norm, out_dtype=F32, row0=lay.rows_p, rows=lay.rows_s)
    return y_prompt.reshape(bp, lp, d), y_sample.reshape(bs, ls, d)
```
